```python
import math
import jax, jax.numpy as jnp
from jax import lax
import numpy as np

D_MODEL = 1024
BATCH = 8
SEQ = 2048
DEPTH = 4
DEC_BATCH = 2
DEC_SEQ = 8192
PAST_LEN = 128

N_MIXERS = 3
GRID_W = 64
Q_BLOCK = 128
RMS_EPS = 1e-6
D_FF = 2816

REL_BUCKETS = 32
REL_MAX_DIST = 128
N_BIAS_HEADS = 16

MLA_HEADS = 16
MLA_Q_LORA = 512
MLA_KV_LORA = 256
MLA_NOPE = 64
MLA_ROPE = 32
MLA_V = 64
ROPE_THETA = 10000.0

DIFF_HEADS = 8
DIFF_QK = 64
DIFF_V = 2 * DIFF_QK

NA_HEADS = 16
NA_HEAD_DIM = 64
NA_KR = 8
NA_KC = 16

kernel_name = 'hybrid_mla_diff_natten_macaron_encoder'


def rms_norm(x, g):
    xf = x.astype(jnp.float32)
    y = xf * lax.rsqrt(jnp.mean(xf * xf, axis=-1, keepdims=True) + RMS_EPS)
    return (y * g.astype(jnp.float32)).astype(x.dtype)


def swiglu(h, w_gate, w_up, w_down):
    return (jax.nn.silu(h @ w_gate) * (h @ w_up)) @ w_down


def rope(x, pos):
    half = x.shape[-1] // 2
    freqs = ROPE_THETA ** (-jnp.arange(half, dtype=jnp.float32) / half)
    ang = pos.astype(jnp.float32)[:, None] * freqs[None, :]
    cos = jnp.cos(ang)[:, None, :]
    sin = jnp.sin(ang)[:, None, :]
    x1 = x[..., :half].astype(jnp.float32)
    x2 = x[..., half:].astype(jnp.float32)
    return jnp.concatenate([x1 * cos - x2 * sin, x1 * sin + x2 * cos], axis=-1).astype(x.dtype)


def t5_bucket(rel):
    half = REL_BUCKETS // 2
    max_exact = half // 2
    n = jnp.abs(rel)
    nf = jnp.maximum(n, max_exact).astype(jnp.float32)
    big = max_exact + (jnp.log(nf / max_exact) / math.log(REL_MAX_DIST / max_exact)
                       * (half - max_exact)).astype(jnp.int32)
    big = jnp.minimum(big, half - 1)
    return jnp.where(rel > 0, half, 0) + jnp.where(n < max_exact, n, big)


def t5_bias_block(table, q_start, n_keys):
    qpos = q_start + jnp.arange(Q_BLOCK)
    kpos = jnp.arange(n_keys)
    b = t5_bucket(kpos[None, :] - qpos[:, None])
    return jnp.transpose(table[b], (2, 0, 1)).astype(jnp.float32)


def dense_attention_blocks(q, k, table, mix):
    B, N, Hm, dk = q.shape
    nb = N // Q_BLOCK
    qb = q.reshape(B, nb, Q_BLOCK, Hm, dk).transpose(1, 0, 2, 3, 4)

    def one(args):
        i, q_i = args
        s = jnp.einsum('bqhd,bkhd->bhqk', q_i, k).astype(jnp.float32)
        s = s + t5_bias_block(table, i * Q_BLOCK, N)[None]
        return mix(jax.nn.softmax(s, axis=-1))

    out = lax.map(one, (jnp.arange(nb), qb))
    return out.transpose(1, 0, 2, 3, 4).reshape(B, N, out.shape[-2], out.shape[-1])


def mla_mixer(h, w_dq, g_q, w_uq, w_dkv, g_kv, w_uk, w_uv, w_o, table):
    B, N, _ = h.shape
    pos = jnp.arange(N)
    c_q = rms_norm(h @ w_dq, g_q)
    q = (c_q @ w_uq).reshape(B, N, MLA_HEADS, MLA_NOPE + MLA_ROPE)
    q_rope = rope(q[..., MLA_NOPE:], pos)
    kv = h @ w_dkv
    c_kv = rms_norm(kv[..., :MLA_KV_LORA], g_kv)
    k_rope = rope(kv[..., MLA_KV_LORA:][:, :, None, :], pos)
    k_nope = (c_kv @ w_uk).reshape(B, N, MLA_HEADS, MLA_NOPE)
    v = (c_kv @ w_uv).reshape(B, N, MLA_HEADS, MLA_V)
    scale = (MLA_NOPE + MLA_ROPE) ** -0.5
    qf = jnp.concatenate([q[..., :MLA_NOPE], q_rope], axis=-1) * scale
    kf = jnp.concatenate([k_nope, jnp.broadcast_to(k_rope, (B, N, MLA_HEADS, MLA_ROPE))], axis=-1)

    def mix(p):
        return jnp.einsum('bhqk,bkhd->bqhd', p.astype(v.dtype), v)

    o = dense_attention_blocks(qf, kf, table, mix)
    return o.reshape(B, N, MLA_HEADS * MLA_V) @ w_o


def diff_mixer(h, w_q, w_k, w_v, lam_q1, lam_k1, lam_q2, lam_k2, g_sub, w_o, table, lam_init):
    B, N, _ = h.shape
    q = (h @ w_q).reshape(B, N, 2 * DIFF_HEADS, DIFF_QK) * (DIFF_QK ** -0.5)
    k = (h @ w_k).reshape(B, N, 2 * DIFF_HEADS, DIFF_QK)
    v = (h @ w_v).reshape(B, N, DIFF_HEADS, DIFF_V)
    lam = (jnp.exp(jnp.sum(lam_q1.astype(jnp.float32) * lam_k1.astype(jnp.float32)))
           - jnp.exp(jnp.sum(lam_q2.astype(jnp.float32) * lam_k2.astype(jnp.float32)))
           + lam_init)

    def mix(p):
        p = p.reshape(B, DIFF_HEADS, 2, Q_BLOCK, N)
        a = p[:, :, 0] - lam * p[:, :, 1]
        return jnp.einsum('bhqk,bkhd->bqhd', a.astype(v.dtype), v)

    o = dense_attention_blocks(q, k, table, mix)
    o = rms_norm(o, g_sub) * (1.0 - lam_init)
    return o.reshape(B, N, DIFF_HEADS * DIFF_V) @ w_o


def na_mixer(h, w_qkv, rpb, w_o):
    B, N, _ = h.shape
    rows = N // GRID_W
    kr = min(NA_KR, rows)
    kc = NA_KC
    qkv = (h @ w_qkv).reshape(B, rows, GRID_W, 3, NA_HEADS, NA_HEAD_DIM)
    q = qkv[:, :, :, 0] * (NA_HEAD_DIM ** -0.5)
    k = qkv[:, :, :, 1]
    v = qkv[:, :, :, 2]
    c = jnp.arange(GRID_W)
    col_idx = jnp.clip(c - kc // 2, 0, GRID_W - kc)[:, None] + jnp.arange(kc)[None, :]
    col_off = col_idx - c[:, None] + (NA_KC - 1)
    r = jnp.arange(rows)
    row_start = jnp.clip(r - kr // 2, 0, rows - kr)

    def one(args):
        r_i, rs, q_r = args
        k_blk = lax.dynamic_slice_in_dim(k, rs, kr, axis=1)
        v_blk = lax.dynamic_slice_in_dim(v, rs, kr, axis=1)
        k_g = k_blk[:, :, col_idx]
        v_g = v_blk[:, :, col_idx]
        s = jnp.einsum('bchd,bacehd->bhcae', q_r, k_g).astype(jnp.float32)
        row_off = rs + jnp.arange(kr) - r_i + (NA_KR - 1)
        bias = rpb[:, row_off[:, None, None], col_off[None, :, :]]
        s = s + jnp.transpose(bias, (0, 2, 1, 3)).astype(jnp.float32)[None]
        p = jax.nn.softmax(s.reshape(B, NA_HEADS, GRID_W, kr * kc), axis=-1)
        p = p.reshape(B, NA_HEADS, GRID_W, kr, kc)
        return jnp.einsum('bhcae,bacehd->bchd', p.astype(v.dtype), v_g)

    out = lax.map(one, (r, row_start, q.transpose(1, 0, 2, 3, 4)))
    out = out.transpose(1, 0, 2, 3, 4).reshape(B, N, NA_HEADS * NA_HEAD_DIM)
    return out @ w_o


def trunk(x, norm_g, final_g, ffn_w_gate, ffn_w_up, ffn_w_down, rel_bias_table,
          mla_w_dq, mla_g_q, mla_w_uq, mla_w_dkv, mla_g_kv, mla_w_uk, mla_w_uv, mla_w_o,
          diff_w_q, diff_w_k, diff_w_v, diff_lam_q1, diff_lam_k1, diff_lam_q2, diff_lam_k2,
          diff_g_sub, diff_w_o, na_w_qkv, na_rpb, na_w_o):
    for i in range(DEPTH):
        x = x + 0.5 * swiglu(rms_norm(x, norm_g[i, 0]), ffn_w_gate[i, 0], ffn_w_up[i, 0], ffn_w_down[i, 0])
        hn = rms_norm(x, norm_g[i, 1])
        m, j = i % N_MIXERS, i // N_MIXERS
        if m == 0:
            y = mla_mixer(hn, mla_w_dq[j], mla_g_q[j], mla_w_uq[j], mla_w_dkv[j], mla_g_kv[j],
                          mla_w_uk[j], mla_w_uv[j], mla_w_o[j], rel_bias_table)
        elif m == 1:
            lam_init = 0.8 - 0.6 * math.exp(-0.3 * i)
            y = diff_mixer(hn, diff_w_q[j], diff_w_k[j], diff_w_v[j], diff_lam_q1[j], diff_lam_k1[j],
                           diff_lam_q2[j], diff_lam_k2[j], diff_g_sub[j], diff_w_o[j],
                           rel_bias_table, lam_init)
        else:
            y = na_mixer(hn, na_w_qkv[j], na_rpb[j], na_w_o[j])
        x = x + y
        x = x + 0.5 * swiglu(rms_norm(x, norm_g[i, 2]), ffn_w_gate[i, 1], ffn_w_up[i, 1], ffn_w_down[i, 1])
    return rms_norm(x, final_g)


def setup_inputs(seed: int = 0) -> dict:
    key = jax.random.key(seed)
    ks = iter(jax.random.split(key, 40))
    n_a = len(range(0, DEPTH, N_MIXERS))
    n_b = len(range(1, DEPTH, N_MIXERS))
    n_c = len(range(2, DEPTH, N_MIXERS))
    D = D_MODEL

    def w(shape, fan_in):
        return jax.random.normal(next(ks), shape, jnp.float32) * fan_in ** -0.5

    def gain(shape):
        return 1.0 + 0.05 * jax.random.normal(next(ks), shape, jnp.float32)

    def small(shape, s):
        return s * jax.random.normal(next(ks), shape, jnp.float32)

    return {
        'x_prompt': jax.random.normal(next(ks), (BATCH, SEQ, D), jnp.float32),
        'x_sample': jax.random.normal(next(ks), (DEC_BATCH, DEC_SEQ, D), jnp.float32),
        'norm_g': gain((DEPTH, 3, D)),
        'final_g': gain((D,)),
        'ffn_w_gate': w((DEPTH, 2, D, D_FF), D),
        'ffn_w_up': w((DEPTH, 2, D, D_FF), D),
        'ffn_w_down': w((DEPTH, 2, D_FF, D), D_FF),
        'rel_bias_table': small((REL_BUCKETS, N_BIAS_HEADS), 0.5),
        'mla_w_dq': w((n_a, D, MLA_Q_LORA), D),
        'mla_g_q': gain((n_a, MLA_Q_LORA)),
        'mla_w_uq': w((n_a, MLA_Q_LORA, MLA_HEADS * (MLA_NOPE + MLA_ROPE)), MLA_Q_LORA),
        'mla_w_dkv': w((n_a, D, MLA_KV_LORA + MLA_ROPE), D),
        'mla_g_kv': gain((n_a, MLA_KV_LORA)),
        'mla_w_uk': w((n_a, MLA_KV_LORA, MLA_HEADS * MLA_NOPE), MLA_KV_LORA),
        'mla_w_uv': w((n_a, MLA_KV_LORA, MLA_HEADS * MLA_V), MLA_KV_LORA),
        'mla_w_o': w((n_a, MLA_HEADS * MLA_V, D), MLA_HEADS * MLA_V),
        'diff_w_q': w((n_b, D, 2 * DIFF_HEADS * DIFF_QK), D),
        'diff_w_k': w((n_b, D, 2 * DIFF_HEADS * DIFF_QK), D),
        'diff_w_v': w((n_b, D, DIFF_HEADS * DIFF_V), D),
        'diff_lam_q1': small((n_b, DIFF_QK), 0.1),
        'diff_lam_k1': small((n_b, DIFF_QK), 0.1),
        'diff_lam_q2': small((n_b, DIFF_QK), 0.1),
        'diff_lam_k2': small((n_b, DIFF_QK), 0.1),
        'diff_g_sub': gain((n_b, DIFF_V)),
        'diff_w_o': w((n_b, DIFF_HEADS * DIFF_V, D), DIFF_HEADS * DIFF_V),
        'na_w_qkv': w((n_c, D, 3 * NA_HEADS * NA_HEAD_DIM), D),
        'na_rpb': small((n_c, NA_HEADS, 2 * NA_KR - 1, 2 * NA_KC - 1), 0.2),
        'na_w_o': w((n_c, NA_HEADS * NA_HEAD_DIM, D), NA_HEADS * NA_HEAD_DIM),
    }


def reference(x_prompt, x_sample, norm_g, final_g, ffn_w_gate, ffn_w_up, ffn_w_down, rel_bias_table,
              mla_w_dq, mla_g_q, mla_w_uq, mla_w_dkv, mla_g_kv, mla_w_uk, mla_w_uv, mla_w_o,
              diff_w_q, diff_w_k, diff_w_v, diff_lam_q1, diff_lam_k1, diff_lam_q2, diff_lam_k2,
              diff_g_sub, diff_w_o, na_w_qkv, na_rpb, na_w_o):
    y_prompt = trunk(x_prompt, norm_g, final_g, ffn_w_gate, ffn_w_up, ffn_w_down, rel_bias_table,
                     mla_w_dq, mla_g_q, mla_w_uq, mla_w_dkv, mla_g_kv, mla_w_uk, mla_w_uv, mla_w_o,
                     diff_w_q, diff_w_k, diff_w_v, diff_lam_q1, diff_lam_k1, diff_lam_q2, diff_lam_k2,
                     diff_g_sub, diff_w_o, na_w_qkv, na_rpb, na_w_o)
    y_sample = trunk(x_sample, norm_g, final_g, ffn_w_gate, ffn_w_up, ffn_w_down, rel_bias_table,
                     mla_w_dq, mla_g_q, mla_w_uq, mla_w_dkv, mla_g_kv, mla_w_uk, mla_w_uv, mla_w_o,
                     diff_w_q, diff_w_k, diff_w_v, diff_lam_q1, diff_lam_k1, diff_lam_q2, diff_lam_k2,
                     diff_g_sub, diff_w_o, na_w_qkv, na_rpb, na_w_o)
    return (y_prompt, y_sample)
```

```python
import functools
import math

import jax
import jax.numpy as jnp
from jax import lax
from jax.experimental import pallas as pl
from jax.experimental.pallas import tpu as pltpu

F32 = jnp.float32
BF16 = jnp.bfloat16

D_MODEL = 1024
DEPTH = 4
N_MIXERS = 3
GRID_W = 64
RMS_EPS = 1e-6
D_FF = 2816

REL_BUCKETS = 32
N_BIAS_HEADS = 16

MLA_HEADS = 16
MLA_Q_LORA = 512
MLA_KV_LORA = 256
MLA_NOPE = 64
MLA_ROPE = 32
MLA_V = 64
ROPE_THETA = 10000.0

DIFF_HEADS = 8
DIFF_QK = 64
DIFF_V = 2 * DIFF_QK

NA_HEADS = 16
NA_HEAD_DIM = 64
NA_KR = 8
NA_KC = 16

LANES = 128
VMEM_LIMIT_BYTES = 56 * 1024 * 1024

TOK_TILE = 512
FF_CHUNK = 256
ATT_TQ = 512
ATT_TK = TOK_TILE
BIAS_TILE = LANES
PAIR_W = 2 * NA_HEAD_DIM
MASK_VALUE = -1e30

T5_THRESHOLDS = (12, 16, 23, 32, 46, 64, 91)


def _params(*semantics):
    return pltpu.CompilerParams(dimension_semantics=semantics, vmem_limit_bytes=VMEM_LIMIT_BYTES)


def _resident(shape):
    zeros = (0,) * len(shape)
    return pl.BlockSpec(shape, lambda *_: zeros, pipeline_mode=pl.Buffered(1))


def _rms(x, g):
    return x * lax.rsqrt(jnp.mean(x * x, axis=-1, keepdims=True) + RMS_EPS) * g


def _dot(a, b):
    return jnp.dot(a, b, preferred_element_type=F32)


def _dot_nt(a, b):
    return lax.dot_general(a, b, (((1,), (1,)), ((), ())), preferred_element_type=F32)


def _ffn_kernel(x_ref, g_ref, wg_ref, wu_ref, wd_ref, fg_ref, o_ref, *, apply_final):
    x = x_ref[...]
    h = _rms(x, g_ref[...]).astype(BF16)
    acc = jnp.zeros(x.shape, F32)
    for c in range(D_FF // FF_CHUNK):
        gate = _dot(h, wg_ref[c])
        up = _dot(h, wu_ref[c])
        act = (gate / (1.0 + jnp.exp(-gate))) * up
        acc = acc + _dot(act.astype(BF16), wd_ref[c])
    y = x + 0.5 * acc
    if apply_final:
        y = _rms(y, fg_ref[...])
    o_ref[...] = y


def _ffn(x, g, wg, wu, wd, final_g, apply_final):
    t = x.shape[0]
    nc = D_FF // FF_CHUNK
    return pl.pallas_call(
        functools.partial(_ffn_kernel, apply_final=apply_final),
        out_shape=jax.ShapeDtypeStruct(x.shape, F32),
        grid=(t // TOK_TILE,),
        in_specs=[
            pl.BlockSpec((TOK_TILE, D_MODEL), lambda i: (i, 0)),
            _resident((1, D_MODEL)),
            _resident((nc, D_MODEL, FF_CHUNK)),
            _resident((nc, D_MODEL, FF_CHUNK)),
            _resident((nc, FF_CHUNK, D_MODEL)),
            _resident((1, D_MODEL)),
        ],
        out_specs=pl.BlockSpec((TOK_TILE, D_MODEL), lambda i: (i, 0)),
        compiler_params=_params("parallel"),
        name="ffn",
    )(x, g, wg, wu, wd, final_g)


def _mla_proj_kernel(x_ref, g_ref, wdq_ref, gq_ref, wuq_ref, wuqr_ref, wckv_ref, gkv_ref,
                     wkr_ref, wkrr_ref, wuk_ref, wuvt_ref, cos_ref, sin_ref,
                     q_ref, k_ref, vt_ref):
    hn = _rms(x_ref[...], g_ref[...]).astype(BF16)
    cos = cos_ref[...]
    sin = sin_ref[...]
    scale = (MLA_NOPE + MLA_ROPE) ** -0.5

    cq = _rms(_dot(hn, wdq_ref[...]), gq_ref[...]).astype(BF16)
    qa = _dot(cq, wuq_ref[...])
    qb = _dot(cq, wuqr_ref[...])
    for h in range(MLA_HEADS):
        sl = slice(h * LANES, (h + 1) * LANES)
        q_ref[:, sl] = ((qa[:, sl] * cos + qb[:, sl] * sin) * scale).astype(BF16)

    ckv = _rms(_dot(hn, wckv_ref[...]), gkv_ref[...]).astype(BF16)
    k_rope = _dot(hn, wkr_ref[...]) * cos + _dot(hn, wkrr_ref[...]) * sin
    k_nope = _dot(ckv, wuk_ref[...])
    for h in range(MLA_HEADS):
        sl = slice(h * LANES, (h + 1) * LANES)
        k_ref[:, sl] = (k_nope[:, sl] + k_rope).astype(BF16)

    vt_ref[0] = _dot_nt(wuvt_ref[...], ckv).astype(BF16)


def _mla_proj(x, seq_len, g, w):
    t = x.shape[0]
    hw = MLA_HEADS * LANES
    pos_blocks = seq_len // TOK_TILE
    tok = lambda width: pl.BlockSpec((TOK_TILE, width), lambda i: (i, 0))
    pos = pl.BlockSpec((TOK_TILE, LANES), lambda i: (i % pos_blocks, 0))
    return pl.pallas_call(
        _mla_proj_kernel,
        out_shape=(
            jax.ShapeDtypeStruct((t, hw), BF16),
            jax.ShapeDtypeStruct((t, hw), BF16),
            jax.ShapeDtypeStruct((t // TOK_TILE, MLA_HEADS * MLA_V, TOK_TILE), BF16),
        ),
        grid=(t // TOK_TILE,),
        in_specs=[
            tok(D_MODEL),
            _resident((1, D_MODEL)),
            _resident((D_MODEL, MLA_Q_LORA)),
            _resident((1, MLA_Q_LORA)),
            _resident((MLA_Q_LORA, hw)),
            _resident((MLA_Q_LORA, hw)),
            _resident((D_MODEL, MLA_KV_LORA)),
            _resident((1, MLA_KV_LORA)),
            _resident((D_MODEL, LANES)),
            _resident((D_MODEL, LANES)),
            _resident((MLA_KV_LORA, hw)),
            _resident((MLA_HEADS * MLA_V, MLA_KV_LORA)),
            pos,
            pos,
        ],
        out_specs=(
            tok(hw),
            tok(hw),
            pl.BlockSpec((1, MLA_HEADS * MLA_V, TOK_TILE), lambda i: (i, 0, 0)),
        ),
        compiler_params=_params("parallel"),
        name="mla_proj",
    )(x, g, w["wdq"], w["gq"], w["wuq"], w["wuq_rot"], w["wckv"], w["gkv"],
      w["wkr"], w["wkr_rot"], w["wuk"], w["wuvt"], w["cos"], w["sin"])


def _qkv_proj_kernel(x_ref, g_ref, wq_ref, wk_ref, wv_ref, q_ref, k_ref, v_ref, *, q_scale, v_transposed):
    hn = _rms(x_ref[...], g_ref[...]).astype(BF16)
    q_ref[...] = (_dot(hn, wq_ref[...]) * q_scale).astype(BF16)
    k_ref[...] = _dot(hn, wk_ref[...]).astype(BF16)
    if v_transposed:
        v_ref[0] = _dot_nt(wv_ref[...], hn).astype(BF16)
    else:
        v_ref[...] = _dot(hn, wv_ref[...]).astype(BF16)


def _qkv_proj(x, g, wq, wk, wv, q_scale, v_transposed):
    t = x.shape[0]
    tok = pl.BlockSpec((TOK_TILE, D_MODEL), lambda i: (i, 0))
    if v_transposed:
        v_shape = jax.ShapeDtypeStruct((t // TOK_TILE, D_MODEL, TOK_TILE), BF16)
        v_spec = pl.BlockSpec((1, D_MODEL, TOK_TILE), lambda i: (i, 0, 0))
    else:
        v_shape = jax.ShapeDtypeStruct((t, D_MODEL), BF16)
        v_spec = tok
    return pl.pallas_call(
        functools.partial(_qkv_proj_kernel, q_scale=q_scale, v_transposed=v_transposed),
        out_shape=(jax.ShapeDtypeStruct((t, D_MODEL), BF16), jax.ShapeDtypeStruct((t, D_MODEL), BF16), v_shape),
        grid=(t // TOK_TILE,),
        in_specs=[tok, _resident((1, D_MODEL)), _resident((D_MODEL, D_MODEL)),
                  _resident((D_MODEL, D_MODEL)), _resident((D_MODEL, D_MODEL))],
        out_specs=(tok, tok, v_spec),
        compiler_params=_params("parallel"),
        name="qkv_proj",
    )(x, g, wq, wk, wv)


def _out_proj_kernel(a_ref, w_ref, x_ref, o_ref):
    o_ref[...] = x_ref[...] + _dot(a_ref[...], w_ref[...])


def _out_proj(a, w, x):
    t = x.shape[0]
    tok = pl.BlockSpec((TOK_TILE, D_MODEL), lambda i: (i, 0))
    return pl.pallas_call(
        _out_proj_kernel,
        out_shape=jax.ShapeDtypeStruct(x.shape, F32),
        grid=(t // TOK_TILE,),
        in_specs=[tok, _resident((D_MODEL, D_MODEL)), tok],
        out_specs=tok,
        compiler_params=_params("parallel"),
        name="out_proj",
    )(a, w, x)


def _t5_tiles_kernel(tab_ref, o_ref):
    head = pl.program_id(0)
    kk = lax.broadcasted_iota(jnp.int32, (BIAS_TILE, BIAS_TILE), 0)
    qq = lax.broadcasted_iota(jnp.int32, (BIAS_TILE, BIAS_TILE), 1)
    half = REL_BUCKETS // 2
    max_exact = half // 2
    for d in range(5):
        rel = (d - 2) * BIAS_TILE + kk - qq
        n = jnp.where(rel < 0, -rel, rel)
        big = jnp.full(rel.shape, max_exact, jnp.int32)
        for thr in T5_THRESHOLDS:
            big = big + jnp.where(n >= thr, 1, 0)
        bucket = jnp.where(rel > 0, half, 0) + jnp.where(n < max_exact, n, big)
        out = jnp.zeros(rel.shape, F32)
        for b in range(REL_BUCKETS):
            out = jnp.where(bucket == b, tab_ref[b, head], out)
        o_ref[d, 0] = out


def _t5_tiles(table):
    return pl.pallas_call(
        _t5_tiles_kernel,
        out_shape=jax.ShapeDtypeStruct((5, N_BIAS_HEADS, BIAS_TILE, BIAS_TILE), F32),
        grid=(N_BIAS_HEADS,),
        in_specs=[pl.BlockSpec(memory_space=pltpu.SMEM)],
        out_specs=pl.BlockSpec((5, 1, BIAS_TILE, BIAS_TILE), lambda h: (0, h, 0, 0)),
        compiler_params=_params("parallel"),
        name="t5_tiles",
    )(table)


def _na_tiles_kernel(rpb_ref, o_ref):
    head = pl.program_id(0)
    n_col = 2 * NA_KC - 1
    c = lax.broadcasted_iota(jnp.int32, (GRID_W, PAIR_W), 0)
    lane = lax.broadcasted_iota(jnp.int32, (GRID_W, PAIR_W), 1)
    upper = lane >= GRID_W
    kc = jnp.where(upper, lane - GRID_W, lane)
    start = jnp.clip(c - NA_KC // 2, 0, GRID_W - NA_KC)
    valid = (kc >= start) & (kc < start + NA_KC)
    col_off = kc - c + (NA_KC - 1)
    for i in range(2 * NA_KR - 2):
        out = jnp.full(c.shape, MASK_VALUE, F32)
        for co in range(n_col):
            val = jnp.where(upper, rpb_ref[head, (i + 1) * n_col + co], rpb_ref[head, i * n_col + co])
            out = jnp.where(valid & (col_off == co), val, out)
        o_ref[0, i] = out


def _na_tiles(rpb):
    n_row = 2 * NA_KR - 1
    n_col = 2 * NA_KC - 1
    return pl.pallas_call(
        _na_tiles_kernel,
        out_shape=jax.ShapeDtypeStruct((NA_HEADS, n_row - 1, GRID_W, PAIR_W), F32),
        grid=(NA_HEADS,),
        in_specs=[pl.BlockSpec(memory_space=pltpu.SMEM)],
        out_specs=pl.BlockSpec((1, n_row - 1, GRID_W, PAIR_W), lambda h: (h, 0, 0, 0)),
        compiler_params=_params("parallel"),
        name="na_tiles",
    )(rpb.reshape(NA_HEADS, n_row * n_col))


def _dense_attn_kernel(q_ref, k_ref, vt_ref, bias_ref, lq1_ref, lk1_ref, lq2_ref, lk2_ref, gsub_ref,
                       o_ref, m_ref, l_ref, acc_ref, *, diff, lam_init):
    qi = pl.program_id(2)
    n_kb = vt_ref.shape[0]
    sub = ATT_TK // BIAS_TILE
    q = q_ref[...]
    if diff:
        lane = lax.broadcasted_iota(jnp.int32, q.shape, 1)
        zero = jnp.zeros_like(q)
        qs = (jnp.where(lane < DIFF_QK, q, zero), jnp.where(lane >= DIFF_QK, q, zero))
    else:
        qs = (q[:, :LANES], q[:, LANES:])

    m_ref[...] = jnp.full(m_ref.shape, MASK_VALUE, F32)
    l_ref[...] = jnp.zeros(l_ref.shape, F32)
    acc_ref[...] = jnp.zeros(acc_ref.shape, F32)

    def step(j, carry):
        kb = k_ref[pl.ds(pl.multiple_of(j * ATT_TK, ATT_TK), ATT_TK), :]
        vb = vt_ref[j]
        base = (j - qi) * sub + 2
        for t in range(2):
            kt = kb if diff else kb[:, t * LANES:(t + 1) * LANES]
            s = _dot_nt(kt, qs[t])
            bias = jnp.concatenate(
                [jnp.concatenate(
                    [bias_ref[jnp.clip(base + a - b, 0, 4), t] for b in range(ATT_TQ // BIAS_TILE)], axis=1)
                 for a in range(sub)], axis=0)
            s = s + bias
            m_prev = m_ref[t]
            m_new = jnp.maximum(m_prev, jnp.max(s, axis=0, keepdims=True))
            alpha = jnp.exp(m_prev - m_new)
            p = jnp.exp(s - m_new)
            l_ref[t] = alpha * l_ref[t] + jnp.sum(p, axis=0, keepdims=True)
            vv = vb if diff else vb[t * MLA_V:(t + 1) * MLA_V]
            acc_ref[t] = acc_ref[t] * alpha + _dot(vv, p.astype(BF16))
            m_ref[t] = m_new
        return carry

    lax.fori_loop(0, n_kb, step, 0)

    o0 = acc_ref[0] / l_ref[0]
    o1 = acc_ref[1] / l_ref[1]
    if diff:
        lam = (jnp.exp(jnp.sum(lq1_ref[...] * lk1_ref[...], axis=-1, keepdims=True))
               - jnp.exp(jnp.sum(lq2_ref[...] * lk2_ref[...], axis=-1, keepdims=True)) + lam_init)
        o = (o0 - lam * o1).T
        o = _rms(o, gsub_ref[...]) * (1.0 - lam_init)
    else:
        o = jnp.concatenate([o0, o1], axis=0).T
    o_ref[...] = o.astype(BF16)


def _dense_attn(q, k, vt, bias, lam_vecs, gsub, batch, seq_len, diff, lam_init):
    pw = PAIR_W if diff else 2 * LANES
    dv = DIFF_V if diff else MLA_V
    nq = seq_len // ATT_TQ
    nk = seq_len // ATT_TK
    small = pl.BlockSpec((1, DIFF_QK), lambda b, h, i: (0, 0))
    return pl.pallas_call(
        functools.partial(_dense_attn_kernel, diff=diff, lam_init=lam_init),
        out_shape=jax.ShapeDtypeStruct((batch * seq_len, D_MODEL), BF16),
        grid=(batch, N_BIAS_HEADS // 2, nq),
        in_specs=[
            pl.BlockSpec((ATT_TQ, pw), lambda b, h, i: (b * nq + i, h)),
            pl.BlockSpec((seq_len, pw), lambda b, h, i: (b, h)),
            pl.BlockSpec((nk, PAIR_W, ATT_TK), lambda b, h, i: (b, h, 0)),
            pl.BlockSpec((5, 2, BIAS_TILE, BIAS_TILE), lambda b, h, i: (0, h, 0, 0)),
            small, small, small, small,
            pl.BlockSpec((1, DIFF_V), lambda b, h, i: (0, 0)),
        ],
        out_specs=pl.BlockSpec((ATT_TQ, PAIR_W), lambda b, h, i: (b * nq + i, h)),
        scratch_shapes=[
            pltpu.VMEM((2, 1, ATT_TQ), F32),
            pltpu.VMEM((2, 1, ATT_TQ), F32),
            pltpu.VMEM((2, dv, ATT_TQ), F32),
        ],
        compiler_params=_params("parallel", "parallel", "parallel"),
        name="diff_attn" if diff else "mla_attn",
    )(q, k, vt, bias, *lam_vecs, gsub)


def _na_attn_kernel(q_ref, k_ref, v_ref, tiles_ref, o_ref):
    rows = q_ref.shape[0] // GRID_W
    slab = NA_KR * GRID_W
    lane = lax.broadcasted_iota(jnp.int32, (GRID_W, PAIR_W), 1)
    first = lane < NA_HEAD_DIM

    def row(r, carry):
        rs = jnp.clip(r - NA_KR // 2, 0, rows - NA_KR)
        tile0 = rs - r + (NA_KR - 1)
        q = q_ref[pl.ds(pl.multiple_of(r * GRID_W, GRID_W), GRID_W), :]
        kw = k_ref[pl.ds(pl.multiple_of(rs * GRID_W, GRID_W), slab), :]
        vw = v_ref[pl.ds(pl.multiple_of(rs * GRID_W, GRID_W), slab), :]
        zero = jnp.zeros_like(q)
        outs = []
        for t in range(2):
            qt = jnp.where(first if t == 0 else ~first, q, zero)
            s = _dot_nt(qt, kw)
            bias = jnp.concatenate([tiles_ref[t, tile0 + 2 * a] for a in range(NA_KR // 2)], axis=1)
            s = s + bias
            m = jnp.max(s, axis=-1, keepdims=True)
            p = jnp.exp(s - m)
            den = jnp.sum(p, axis=-1, keepdims=True)
            outs.append(_dot(p.astype(BF16), vw) / den)
        o = jnp.where(first, outs[0], outs[1])
        o_ref[pl.ds(pl.multiple_of(r * GRID_W, GRID_W), GRID_W), :] = o.astype(BF16)
        return carry

    lax.fori_loop(0, rows, row, 0)


def _na_attn(q, k, v, tiles, batch, seq_len):
    seq = pl.BlockSpec((seq_len, PAIR_W), lambda b, h: (b, h))
    n_tiles = 2 * NA_KR - 2
    return pl.pallas_call(
        _na_attn_kernel,
        out_shape=jax.ShapeDtypeStruct((batch * seq_len, D_MODEL), BF16),
        grid=(batch, NA_HEADS // 2),
        in_specs=[seq, seq, seq, pl.BlockSpec((2, n_tiles, GRID_W, PAIR_W), lambda b, h: (h, 0, 0, 0))],
        out_specs=seq,
        compiler_params=_params("parallel", "parallel"),
        name="na_attn",
    )(q, k, v, tiles)


def _pad_heads(w, heads, dim):
    w = w.reshape(w.shape[0], heads, dim)
    return jnp.pad(w, ((0, 0), (0, 0), (0, LANES - dim))).reshape(w.shape[0], heads * LANES)


def _rot_half_cols(w_rope):
    half = w_rope.shape[-1] // 2
    return jnp.concatenate([-w_rope[..., half:], w_rope[..., :half]], axis=-1)


def _rope_tables(max_len):
    half = MLA_ROPE // 2
    freqs = ROPE_THETA ** (-jnp.arange(half, dtype=F32) / half)
    ang = jnp.arange(max_len, dtype=F32)[:, None] * freqs[None, :]
    cos = jnp.cos(ang)
    sin = jnp.sin(ang)
    pad = jnp.zeros((max_len, LANES - MLA_NOPE - MLA_ROPE), F32)
    cos_t = jnp.concatenate([jnp.ones((max_len, MLA_NOPE), F32), cos, cos, pad], axis=-1)
    sin_t = jnp.concatenate([jnp.zeros((max_len, MLA_NOPE), F32), sin, sin, pad], axis=-1)
    return cos_t, sin_t


def _mla_weights(w_dq, g_q, w_uq, w_dkv, g_kv, w_uk, w_uv, max_len):
    d_qk = MLA_NOPE + MLA_ROPE
    uq = w_uq.reshape(MLA_Q_LORA, MLA_HEADS, d_qk)
    uq_rot = jnp.concatenate(
        [jnp.zeros((MLA_Q_LORA, MLA_HEADS, MLA_NOPE), F32), _rot_half_cols(uq[..., MLA_NOPE:])], axis=-1)
    w_kr = w_dkv[:, MLA_KV_LORA:]
    place = lambda w: jnp.pad(w, ((0, 0), (MLA_NOPE, LANES - d_qk)))
    cos_t, sin_t = _rope_tables(max_len)
    return {
        "wdq": w_dq.astype(BF16),
        "gq": g_q.reshape(1, -1),
        "wuq": _pad_heads(uq.reshape(MLA_Q_LORA, -1), MLA_HEADS, d_qk).astype(BF16),
        "wuq_rot": _pad_heads(uq_rot.reshape(MLA_Q_LORA, -1), MLA_HEADS, d_qk).astype(BF16),
        "wckv": w_dkv[:, :MLA_KV_LORA].astype(BF16),
        "gkv": g_kv.reshape(1, -1),
        "wkr": place(w_kr).astype(BF16),
        "wkr_rot": place(_rot_half_cols(w_kr)).astype(BF16),
        "wuk": _pad_heads(w_uk, MLA_HEADS, MLA_NOPE).astype(BF16),
        "wuvt": w_uv.T.astype(BF16),
        "cos": cos_t,
        "sin": sin_t,
    }


def _ffn_weights(w_gate, w_up, w_down):
    nc = D_FF // FF_CHUNK
    cols = lambda w: w.reshape(D_MODEL, nc, FF_CHUNK).transpose(1, 0, 2).astype(BF16)
    return cols(w_gate), cols(w_up), w_down.reshape(nc, FF_CHUNK, D_MODEL).astype(BF16)


def _trunk(x, p):
    batch, seq_len, _ = x.shape
    x = x.reshape(batch * seq_len, D_MODEL)
    dummy_vecs = (jnp.zeros((1, DIFF_QK), F32),) * 4
    dummy_g = jnp.ones((1, DIFF_V), F32)
    for i in range(DEPTH):
        x = _ffn(x, p["norm_g"][i, 0].reshape(1, -1), *p["ffn"][i][0], p["final_g"], False)
        g_mix = p["norm_g"][i, 1].reshape(1, -1)
        m, j = i % N_MIXERS, i // N_MIXERS
        if m == 0:
            w = p["mla"][j]
            q, k, vt = _mla_proj(x, seq_len, g_mix, w)
            a = _dense_attn(q, k, vt, p["t5"], dummy_vecs, dummy_g, batch, seq_len, False, 0.0)
            w_o = w["wo"]
        elif m == 1:
            w = p["diff"][j]
            lam_init = 0.8 - 0.6 * math.exp(-0.3 * i)
            q, k, vt = _qkv_proj(x, g_mix, w["wq"], w["wk"], w["wvt"], DIFF_QK ** -0.5, True)
            a = _dense_attn(q, k, vt, p["t5"], w["lam"], w["gsub"], batch, seq_len, True, lam_init)
            w_o = w["wo"]
        else:
            w = p["na"][j]
            q, k, v = _qkv_proj(x, g_mix, w["wq"], w["wk"], w["wv"], NA_HEAD_DIM ** -0.5, False)
            a = _na_attn(q, k, v, w["tiles"], batch, seq_len)
            w_o = w["wo"]
        x = _out_proj(a, w_o, x)
        x = _ffn(x, p["norm_g"][i, 2].reshape(1, -1), *p["ffn"][i][1], p["final_g"], i == DEPTH - 1)
    return x.reshape(batch, seq_len, D_MODEL)


def kernel(x_prompt, x_sample, norm_g, final_g, ffn_w_gate, ffn_w_up, ffn_w_down, rel_bias_table, mla_w_dq, mla_g_q, mla_w_uq, mla_w_dkv, mla_g_kv, mla_w_uk, mla_w_uv, mla_w_o, diff_w_q, diff_w_k, diff_w_v, diff_lam_q1, diff_lam_k1, diff_lam_q2, diff_lam_k2, diff_g_sub, diff_w_o, na_w_qkv, na_rpb, na_w_o):
    max_len = max(x_prompt.shape[1], x_sample.shape[1])
    hd = NA_HEADS * NA_HEAD_DIM
    p = {
        "norm_g": norm_g,
        "final_g": final_g.reshape(1, -1),
        "ffn": [[_ffn_weights(ffn_w_gate[i, s], ffn_w_up[i, s], ffn_w_down[i, s]) for s in range(2)]
                for i in range(DEPTH)],
        "t5": _t5_tiles(rel_bias_table),
        "mla": [dict(_mla_weights(mla_w_dq[j], mla_g_q[j], mla_w_uq[j], mla_w_dkv[j], mla_g_kv[j],
                                  mla_w_uk[j], mla_w_uv[j], max_len), wo=mla_w_o[j].astype(BF16))
                for j in range(mla_w_dq.shape[0])],
        "diff": [{
            "wq": diff_w_q[j].astype(BF16), "wk": diff_w_k[j].astype(BF16), "wvt": diff_w_v[j].T.astype(BF16),
            "lam": tuple(v[j].reshape(1, -1) for v in (diff_lam_q1, diff_lam_k1, diff_lam_q2, diff_lam_k2)),
            "gsub": diff_g_sub[j].reshape(1, -1), "wo": diff_w_o[j].astype(BF16),
        } for j in range(diff_w_q.shape[0])],
        "na": [{
            "wq": na_w_qkv[j][:, :hd].astype(BF16), "wk": na_w_qkv[j][:, hd:2 * hd].astype(BF16),
            "wv": na_w_qkv[j][:, 2 * hd:].astype(BF16), "tiles": _na_tiles(na_rpb[j]),
            "wo": na_w_o[j].astype(BF16),
        } for j in range(na_w_qkv.shape[0])],
    }
    return (_trunk(x_prompt, p), _trunk(x_sample, p))
```

```python
import functools
import math

import jax
import jax.numpy as jnp
from jax import lax
from jax.experimental import pallas as pl
from jax.experimental.pallas import tpu as pltpu

F32 = jnp.float32
BF16 = jnp.bfloat16

D_MODEL = 1024
DEPTH = 4
N_MIXERS = 3
GRID_W = 64
RMS_EPS = 1e-6
D_FF = 2816

REL_BUCKETS = 32
N_BIAS_HEADS = 16

MLA_HEADS = 16
MLA_Q_LORA = 512
MLA_KV_LORA = 256
MLA_NOPE = 64
MLA_ROPE = 32
MLA_V = 64
ROPE_THETA = 10000.0

DIFF_HEADS = 8
DIFF_QK = 64
DIFF_V = 2 * DIFF_QK

NA_HEADS = 16
NA_HEAD_DIM = 64
NA_KR = 8
NA_KC = 16

LANES = 128
VMEM_LIMIT_BYTES = 56 * 1024 * 1024

TOK_TILE = 512
FF_CHUNK = 256
ATT_TQ = 512
ATT_TK = TOK_TILE
BIAS_TILE = LANES
PAIR_W = 2 * NA_HEAD_DIM
MASK_VALUE = -1e30
ONES_ROWS = 16
LOG2E = math.log2(math.e)

T5_THRESHOLDS = (12, 16, 23, 32, 46, 64, 91)


def _params(*semantics):
    return pltpu.CompilerParams(dimension_semantics=semantics, vmem_limit_bytes=VMEM_LIMIT_BYTES)


def _resident(shape):
    zeros = (0,) * len(shape)
    return pl.BlockSpec(shape, lambda *_: zeros, pipeline_mode=pl.Buffered(1))


def _rms(x, g):
    return x * lax.rsqrt(jnp.mean(x * x, axis=-1, keepdims=True) + RMS_EPS) * g


def _dot(a, b):
    return jnp.dot(a, b, preferred_element_type=F32)


def _dot_nt(a, b):
    return lax.dot_general(a, b, (((1,), (1,)), ((), ())), preferred_element_type=F32)


def _ffn_kernel(x_ref, g_ref, wg_ref, wu_ref, wd_ref, fg_ref, o_ref, *, apply_final):
    x = x_ref[...]
    h = _rms(x, g_ref[...]).astype(BF16)
    acc = jnp.zeros(x.shape, F32)
    for c in range(D_FF // FF_CHUNK):
        gate = _dot(h, wg_ref[c])
        up = _dot(h, wu_ref[c])
        act = (gate / (1.0 + jnp.exp(-gate))) * up
        acc = acc + _dot(act.astype(BF16), wd_ref[c])
    y = x + 0.5 * acc
    if apply_final:
        y = _rms(y, fg_ref[...])
    o_ref[...] = y


def _ffn(x, g, wg, wu, wd, final_g, apply_final):
    t = x.shape[0]
    nc = D_FF // FF_CHUNK
    return pl.pallas_call(
        functools.partial(_ffn_kernel, apply_final=apply_final),
        out_shape=jax.ShapeDtypeStruct(x.shape, F32),
        grid=(t // TOK_TILE,),
        in_specs=[
            pl.BlockSpec((TOK_TILE, D_MODEL), lambda i: (i, 0)),
            _resident((1, D_MODEL)),
            _resident((nc, D_MODEL, FF_CHUNK)),
            _resident((nc, D_MODEL, FF_CHUNK)),
            _resident((nc, FF_CHUNK, D_MODEL)),
            _resident((1, D_MODEL)),
        ],
        out_specs=pl.BlockSpec((TOK_TILE, D_MODEL), lambda i: (i, 0)),
        compiler_params=_params("parallel"),
        name="ffn",
    )(x, g, wg, wu, wd, final_g)


def _mla_proj_kernel(x_ref, g_ref, wdq_ref, gq_ref, wuq_ref, wuqr_ref, wckv_ref, gkv_ref,
                     wkr_ref, wkrr_ref, wuk_ref, wuvt_ref, cos_ref, sin_ref,
                     q_ref, k_ref, vt_ref):
    hn = _rms(x_ref[...], g_ref[...]).astype(BF16)
    cos = cos_ref[...]
    sin = sin_ref[...]
    scale = (MLA_NOPE + MLA_ROPE) ** -0.5 * LOG2E

    cq = _rms(_dot(hn, wdq_ref[...]), gq_ref[...]).astype(BF16)
    qa = _dot(cq, wuq_ref[...])
    qb = _dot(cq, wuqr_ref[...])
    for h in range(MLA_HEADS):
        sl = slice(h * LANES, (h + 1) * LANES)
        q_ref[:, sl] = ((qa[:, sl] * cos + qb[:, sl] * sin) * scale).astype(BF16)

    ckv = _rms(_dot(hn, wckv_ref[...]), gkv_ref[...]).astype(BF16)
    k_rope = _dot(hn, wkr_ref[...]) * cos + _dot(hn, wkrr_ref[...]) * sin
    k_nope = _dot(ckv, wuk_ref[...])
    for h in range(MLA_HEADS):
        sl = slice(h * LANES, (h + 1) * LANES)
        k_ref[:, sl] = (k_nope[:, sl] + k_rope).astype(BF16)

    vt_ref[0] = _dot_nt(wuvt_ref[...], ckv).astype(BF16)


def _mla_proj(x, seq_len, g, w):
    t = x.shape[0]
    hw = MLA_HEADS * LANES
    pos_blocks = seq_len // TOK_TILE
    tok = lambda width: pl.BlockSpec((TOK_TILE, width), lambda i: (i, 0))
    pos = pl.BlockSpec((TOK_TILE, LANES), lambda i: (i % pos_blocks, 0))
    return pl.pallas_call(
        _mla_proj_kernel,
        out_shape=(
            jax.ShapeDtypeStruct((t, hw), BF16),
            jax.ShapeDtypeStruct((t, hw), BF16),
            jax.ShapeDtypeStruct((t // TOK_TILE, MLA_HEADS * MLA_V, TOK_TILE), BF16),
        ),
        grid=(t // TOK_TILE,),
        in_specs=[
            tok(D_MODEL),
            _resident((1, D_MODEL)),
            _resident((D_MODEL, MLA_Q_LORA)),
            _resident((1, MLA_Q_LORA)),
            _resident((MLA_Q_LORA, hw)),
            _resident((MLA_Q_LORA, hw)),
            _resident((D_MODEL, MLA_KV_LORA)),
            _resident((1, MLA_KV_LORA)),
            _resident((D_MODEL, LANES)),
            _resident((D_MODEL, LANES)),
            _resident((MLA_KV_LORA, hw)),
            _resident((MLA_HEADS * MLA_V, MLA_KV_LORA)),
            pos,
            pos,
        ],
        out_specs=(
            tok(hw),
            tok(hw),
            pl.BlockSpec((1, MLA_HEADS * MLA_V, TOK_TILE), lambda i: (i, 0, 0)),
        ),
        compiler_params=_params("parallel"),
        name="mla_proj",
    )(x, g, w["wdq"], w["gq"], w["wuq"], w["wuq_rot"], w["wckv"], w["gkv"],
      w["wkr"], w["wkr_rot"], w["wuk"], w["wuvt"], w["cos"], w["sin"])


def _qkv_proj_kernel(x_ref, g_ref, wq_ref, wk_ref, wv_ref, q_ref, k_ref, v_ref, *, q_scale, v_transposed):
    hn = _rms(x_ref[...], g_ref[...]).astype(BF16)
    q_ref[...] = (_dot(hn, wq_ref[...]) * q_scale).astype(BF16)
    k_ref[...] = _dot(hn, wk_ref[...]).astype(BF16)
    if v_transposed:
        v_ref[0] = _dot_nt(wv_ref[...], hn).astype(BF16)
    else:
        v_ref[...] = _dot(hn, wv_ref[...]).astype(BF16)


def _qkv_proj(x, g, wq, wk, wv, q_scale, v_transposed):
    t = x.shape[0]
    tok = pl.BlockSpec((TOK_TILE, D_MODEL), lambda i: (i, 0))
    if v_transposed:
        v_shape = jax.ShapeDtypeStruct((t // TOK_TILE, D_MODEL, TOK_TILE), BF16)
        v_spec = pl.BlockSpec((1, D_MODEL, TOK_TILE), lambda i: (i, 0, 0))
    else:
        v_shape = jax.ShapeDtypeStruct((t, D_MODEL), BF16)
        v_spec = tok
    return pl.pallas_call(
        functools.partial(_qkv_proj_kernel, q_scale=q_scale, v_transposed=v_transposed),
        out_shape=(jax.ShapeDtypeStruct((t, D_MODEL), BF16), jax.ShapeDtypeStruct((t, D_MODEL), BF16), v_shape),
        grid=(t // TOK_TILE,),
        in_specs=[tok, _resident((1, D_MODEL)), _resident((D_MODEL, D_MODEL)),
                  _resident((D_MODEL, D_MODEL)), _resident((D_MODEL, D_MODEL))],
        out_specs=(tok, tok, v_spec),
        compiler_params=_params("parallel"),
        name="qkv_proj",
    )(x, g, wq, wk, wv)


def _out_proj_kernel(a_ref, w_ref, x_ref, o_ref):
    o_ref[...] = x_ref[...] + _dot(a_ref[...], w_ref[...])


def _out_proj(a, w, x):
    t = x.shape[0]
    tok = pl.BlockSpec((TOK_TILE, D_MODEL), lambda i: (i, 0))
    return pl.pallas_call(
        _out_proj_kernel,
        out_shape=jax.ShapeDtypeStruct(x.shape, F32),
        grid=(t // TOK_TILE,),
        in_specs=[tok, _resident((D_MODEL, D_MODEL)), tok],
        out_specs=tok,
        compiler_params=_params("parallel"),
        name="out_proj",
    )(a, w, x)


def _t5_tiles_kernel(tab_ref, o_ref):
    head = pl.program_id(0)
    kk = lax.broadcasted_iota(jnp.int32, (BIAS_TILE, BIAS_TILE), 0)
    qq = lax.broadcasted_iota(jnp.int32, (BIAS_TILE, BIAS_TILE), 1)
    half = REL_BUCKETS // 2
    max_exact = half // 2
    for d in range(5):
        rel = (d - 2) * BIAS_TILE + kk - qq
        n = jnp.where(rel < 0, -rel, rel)
        big = jnp.full(rel.shape, max_exact, jnp.int32)
        for thr in T5_THRESHOLDS:
            big = big + jnp.where(n >= thr, 1, 0)
        bucket = jnp.where(rel > 0, half, 0) + jnp.where(n < max_exact, n, big)
        out = jnp.zeros(rel.shape, F32)
        for b in range(REL_BUCKETS):
            out = jnp.where(bucket == b, tab_ref[b, head], out)
        o_ref[d, 0] = out * LOG2E


def _t5_tiles(table):
    return pl.pallas_call(
        _t5_tiles_kernel,
        out_shape=jax.ShapeDtypeStruct((5, N_BIAS_HEADS, BIAS_TILE, BIAS_TILE), F32),
        grid=(N_BIAS_HEADS,),
        in_specs=[pl.BlockSpec(memory_space=pltpu.SMEM)],
        out_specs=pl.BlockSpec((5, 1, BIAS_TILE, BIAS_TILE), lambda h: (0, h, 0, 0)),
        compiler_params=_params("parallel"),
        name="t5_tiles",
    )(table)


def _na_tiles_kernel(rpb_ref, o_ref):
    head = pl.program_id(0)
    n_col = 2 * NA_KC - 1
    c = lax.broadcasted_iota(jnp.int32, (GRID_W, PAIR_W), 0)
    lane = lax.broadcasted_iota(jnp.int32, (GRID_W, PAIR_W), 1)
    upper = lane >= GRID_W
    kc = jnp.where(upper, lane - GRID_W, lane)
    start = jnp.clip(c - NA_KC // 2, 0, GRID_W - NA_KC)
    valid = (kc >= start) & (kc < start + NA_KC)
    col_off = kc - c + (NA_KC - 1)
    for i in range(2 * NA_KR - 2):
        out = jnp.full(c.shape, MASK_VALUE, F32)
        for co in range(n_col):
            val = jnp.where(upper, rpb_ref[head, (i + 1) * n_col + co], rpb_ref[head, i * n_col + co])
            out = jnp.where(valid & (col_off == co), val, out)
        o_ref[0, i] = out


def _na_tiles(rpb):
    n_row = 2 * NA_KR - 1
    n_col = 2 * NA_KC - 1
    return pl.pallas_call(
        _na_tiles_kernel,
        out_shape=jax.ShapeDtypeStruct((NA_HEADS, n_row - 1, GRID_W, PAIR_W), F32),
        grid=(NA_HEADS,),
        in_specs=[pl.BlockSpec(memory_space=pltpu.SMEM)],
        out_specs=pl.BlockSpec((1, n_row - 1, GRID_W, PAIR_W), lambda h: (h, 0, 0, 0)),
        compiler_params=_params("parallel"),
        name="na_tiles",
    )(rpb.reshape(NA_HEADS, n_row * n_col))


def _dense_attn_kernel(q_ref, k_ref, vt_ref, bias_ref, lq1_ref, lk1_ref, lq2_ref, lk2_ref, gsub_ref,
                       o_ref, qm_ref, sa_ref, sb_ref, mba_ref, mbb_ref, m_ref, acc_ref, *, diff, lam_init):
    qi = pl.program_id(2)
    n_kb = vt_ref.shape[0]
    sub = ATT_TK // BIAS_TILE
    dv = DIFF_V if diff else MLA_V

    q = q_ref[...]
    if diff:
        lane = lax.broadcasted_iota(jnp.int32, q.shape, 1)
        zero = jnp.zeros_like(q)
        qm_ref[0] = jnp.where(lane < DIFF_QK, q, zero)
        qm_ref[1] = jnp.where(lane >= DIFF_QK, q, zero)
    else:
        qm_ref[0] = q[:, :LANES]
        qm_ref[1] = q[:, LANES:]
    m_ref[...] = jnp.full(m_ref.shape, MASK_VALUE, F32)
    acc_ref[...] = jnp.zeros(acc_ref.shape, F32)
    ones = jnp.ones((ONES_ROWS, ATT_TK), BF16)

    def scores(j, s_ref, mb_ref):
        kb = k_ref[pl.ds(pl.multiple_of(j * ATT_TK, ATT_TK), ATT_TK), :]
        base = (j - qi) * sub + 2
        for t in range(2):
            kt = kb if diff else kb[:, t * LANES:(t + 1) * LANES]
            bias = jnp.concatenate(
                [jnp.concatenate(
                    [bias_ref[jnp.clip(base + a - b, 0, 4), t] for b in range(ATT_TQ // BIAS_TILE)], axis=1)
                 for a in range(sub)], axis=0)
            s = _dot_nt(kt, qm_ref[t]) + bias
            s_ref[t] = s
            mb_ref[t] = jnp.max(s, axis=0, keepdims=True)

    def update(j, s_ref, mb_ref):
        vb = vt_ref[j]
        for t in range(2):
            m_prev = m_ref[t]
            m_new = jnp.maximum(m_prev, mb_ref[t])
            alpha = jnp.exp2(m_prev - m_new)
            p = jnp.exp2(s_ref[t] - m_new).astype(BF16)
            vv = vb if diff else vb[t * MLA_V:(t + 1) * MLA_V]
            acc_ref[t] = acc_ref[t] * alpha + _dot(jnp.concatenate([vv, ones], axis=0), p)
            m_ref[t] = m_new

    scores(0, sa_ref, mba_ref)

    def two_blocks(jj, carry):
        j = 2 * jj
        scores(j + 1, sb_ref, mbb_ref)
        update(j, sa_ref, mba_ref)
        scores(j + 2, sa_ref, mba_ref)
        update(j + 1, sb_ref, mbb_ref)
        return carry

    lax.fori_loop(0, n_kb // 2 - 1, two_blocks, 0)
    scores(n_kb - 1, sb_ref, mbb_ref)
    update(n_kb - 2, sa_ref, mba_ref)
    update(n_kb - 1, sb_ref, mbb_ref)

    a0 = acc_ref[0]
    a1 = acc_ref[1]
    o0 = a0[:dv] / a0[dv:dv + 1]
    o1 = a1[:dv] / a1[dv:dv + 1]
    if diff:
        lam = (jnp.exp(jnp.sum(lq1_ref[...] * lk1_ref[...], axis=-1, keepdims=True))
               - jnp.exp(jnp.sum(lq2_ref[...] * lk2_ref[...], axis=-1, keepdims=True)) + lam_init)
        o = (o0 - lam * o1).T
        o = _rms(o, gsub_ref[...]) * (1.0 - lam_init)
    else:
        o = jnp.concatenate([o0, o1], axis=0).T
    o_ref[...] = o.astype(BF16)


def _dense_attn(q, k, vt, bias, lam_vecs, gsub, batch, seq_len, diff, lam_init):
    pw = PAIR_W if diff else 2 * LANES
    dv = DIFF_V if diff else MLA_V
    nq = seq_len // ATT_TQ
    nk = seq_len // ATT_TK
    assert nk % 2 == 0 and nk >= 2, "key blocks are consumed two per loop trip"
    small = pl.BlockSpec((1, DIFF_QK), lambda b, h, i: (0, 0))
    return pl.pallas_call(
        functools.partial(_dense_attn_kernel, diff=diff, lam_init=lam_init),
        out_shape=jax.ShapeDtypeStruct((batch * seq_len, D_MODEL), BF16),
        grid=(batch, N_BIAS_HEADS // 2, nq),
        in_specs=[
            pl.BlockSpec((ATT_TQ, pw), lambda b, h, i: (b * nq + i, h)),
            pl.BlockSpec((seq_len, pw), lambda b, h, i: (b, h)),
            pl.BlockSpec((nk, PAIR_W, ATT_TK), lambda b, h, i: (b, h, 0)),
            pl.BlockSpec((5, 2, BIAS_TILE, BIAS_TILE), lambda b, h, i: (0, h, 0, 0)),
            small, small, small, small,
            pl.BlockSpec((1, DIFF_V), lambda b, h, i: (0, 0)),
        ],
        out_specs=pl.BlockSpec((ATT_TQ, PAIR_W), lambda b, h, i: (b * nq + i, h)),
        scratch_shapes=[
            pltpu.VMEM((2, ATT_TQ, LANES), BF16),
            pltpu.VMEM((2, ATT_TK, ATT_TQ), F32),
            pltpu.VMEM((2, ATT_TK, ATT_TQ), F32),
            pltpu.VMEM((2, 1, ATT_TQ), F32),
            pltpu.VMEM((2, 1, ATT_TQ), F32),
            pltpu.VMEM((2, 1, ATT_TQ), F32),
            pltpu.VMEM((2, dv + ONES_ROWS, ATT_TQ), F32),
        ],
        compiler_params=_params("parallel", "parallel", "parallel"),
        name="diff_attn" if diff else "mla_attn",
    )(q, k, vt, bias, *lam_vecs, gsub)


def _na_attn_kernel(q_ref, k_ref, v_ref, tiles_ref, o_ref):
    rows = q_ref.shape[0] // GRID_W
    slab = NA_KR * GRID_W
    lane = lax.broadcasted_iota(jnp.int32, (GRID_W, PAIR_W), 1)
    first = lane < NA_HEAD_DIM

    def row(r, carry):
        rs = jnp.clip(r - NA_KR // 2, 0, rows - NA_KR)
        tile0 = rs - r + (NA_KR - 1)
        q = q_ref[pl.ds(pl.multiple_of(r * GRID_W, GRID_W), GRID_W), :]
        kw = k_ref[pl.ds(pl.multiple_of(rs * GRID_W, GRID_W), slab), :]
        vw = v_ref[pl.ds(pl.multiple_of(rs * GRID_W, GRID_W), slab), :]
        zero = jnp.zeros_like(q)
        outs = []
        for t in range(2):
            qt = jnp.where(first if t == 0 else ~first, q, zero)
            s = _dot_nt(qt, kw)
            bias = jnp.concatenate([tiles_ref[t, tile0 + 2 * a] for a in range(NA_KR // 2)], axis=1)
            s = s + bias
            m = jnp.max(s, axis=-1, keepdims=True)
            p = jnp.exp(s - m)
            den = jnp.sum(p, axis=-1, keepdims=True)
            outs.append(_dot(p.astype(BF16), vw) / den)
        o = jnp.where(first, outs[0], outs[1])
        o_ref[pl.ds(pl.multiple_of(r * GRID_W, GRID_W), GRID_W), :] = o.astype(BF16)
        return carry

    lax.fori_loop(0, rows, row, 0)


def _na_attn(q, k, v, tiles, batch, seq_len):
    seq = pl.BlockSpec((seq_len, PAIR_W), lambda b, h: (b, h))
    n_tiles = 2 * NA_KR - 2
    return pl.pallas_call(
        _na_attn_kernel,
        out_shape=jax.ShapeDtypeStruct((batch * seq_len, D_MODEL), BF16),
        grid=(batch, NA_HEADS // 2),
        in_specs=[seq, seq, seq, pl.BlockSpec((2, n_tiles, GRID_W, PAIR_W), lambda b, h: (h, 0, 0, 0))],
        out_specs=seq,
        compiler_params=_params("parallel", "parallel"),
        name="na_attn",
    )(q, k, v, tiles)


def _pad_heads(w, heads, dim):
    w = w.reshape(w.shape[0], heads, dim)
    return jnp.pad(w, ((0, 0), (0, 0), (0, LANES - dim))).reshape(w.shape[0], heads * LANES)


def _rot_half_cols(w_rope):
    half = w_rope.shape[-1] // 2
    return jnp.concatenate([-w_rope[..., half:], w_rope[..., :half]], axis=-1)


def _rope_tables(max_len):
    half = MLA_ROPE // 2
    freqs = ROPE_THETA ** (-jnp.arange(half, dtype=F32) / half)
    ang = jnp.arange(max_len, dtype=F32)[:, None] * freqs[None, :]
    cos = jnp.cos(ang)
    sin = jnp.sin(ang)
    pad = jnp.zeros((max_len, LANES - MLA_NOPE - MLA_ROPE), F32)
    cos_t = jnp.concatenate([jnp.ones((max_len, MLA_NOPE), F32), cos, cos, pad], axis=-1)
    sin_t = jnp.concatenate([jnp.zeros((max_len, MLA_NOPE), F32), sin, sin, pad], axis=-1)
    return cos_t, sin_t


def _mla_weights(w_dq, g_q, w_uq, w_dkv, g_kv, w_uk, w_uv, max_len):
    d_qk = MLA_NOPE + MLA_ROPE
    uq = w_uq.reshape(MLA_Q_LORA, MLA_HEADS, d_qk)
    uq_rot = jnp.concatenate(
        [jnp.zeros((MLA_Q_LORA, MLA_HEADS, MLA_NOPE), F32), _rot_half_cols(uq[..., MLA_NOPE:])], axis=-1)
    w_kr = w_dkv[:, MLA_KV_LORA:]
    place = lambda w: jnp.pad(w, ((0, 0), (MLA_NOPE, LANES - d_qk)))
    cos_t, sin_t = _rope_tables(max_len)
    return {
        "wdq": w_dq.astype(BF16),
        "gq": g_q.reshape(1, -1),
        "wuq": _pad_heads(uq.reshape(MLA_Q_LORA, -1), MLA_HEADS, d_qk).astype(BF16),
        "wuq_rot": _pad_heads(uq_rot.reshape(MLA_Q_LORA, -1), MLA_HEADS, d_qk).astype(BF16),
        "wckv": w_dkv[:, :MLA_KV_LORA].astype(BF16),
        "gkv": g_kv.reshape(1, -1),
        "wkr": place(w_kr).astype(BF16),
        "wkr_rot": place(_rot_half_cols(w_kr)).astype(BF16),
        "wuk": _pad_heads(w_uk, MLA_HEADS, MLA_NOPE).astype(BF16),
        "wuvt": w_uv.T.astype(BF16),
        "cos": cos_t,
        "sin": sin_t,
    }


def _ffn_weights(w_gate, w_up, w_down):
    nc = D_FF // FF_CHUNK
    cols = lambda w: w.reshape(D_MODEL, nc, FF_CHUNK).transpose(1, 0, 2).astype(BF16)
    return cols(w_gate), cols(w_up), w_down.reshape(nc, FF_CHUNK, D_MODEL).astype(BF16)


def _trunk(x, p):
    batch, seq_len, _ = x.shape
    x = x.reshape(batch * seq_len, D_MODEL)
    dummy_vecs = (jnp.zeros((1, DIFF_QK), F32),) * 4
    dummy_g = jnp.ones((1, DIFF_V), F32)
    for i in range(DEPTH):
        x = _ffn(x, p["norm_g"][i, 0].reshape(1, -1), *p["ffn"][i][0], p["final_g"], False)
        g_mix = p["norm_g"][i, 1].reshape(1, -1)
        m, j = i % N_MIXERS, i // N_MIXERS
        if m == 0:
            w = p["mla"][j]
            q, k, vt = _mla_proj(x, seq_len, g_mix, w)
            a = _dense_attn(q, k, vt, p["t5"], dummy_vecs, dummy_g, batch, seq_len, False, 0.0)
            w_o = w["wo"]
        elif m == 1:
            w = p["diff"][j]
            lam_init = 0.8 - 0.6 * math.exp(-0.3 * i)
            q, k, vt = _qkv_proj(x, g_mix, w["wq"], w["wk"], w["wvt"], DIFF_QK ** -0.5 * LOG2E, True)
            a = _dense_attn(q, k, vt, p["t5"], w["lam"], w["gsub"], batch, seq_len, True, lam_init)
            w_o = w["wo"]
        else:
            w = p["na"][j]
            q, k, v = _qkv_proj(x, g_mix, w["wq"], w["wk"], w["wv"], NA_HEAD_DIM ** -0.5, False)
            a = _na_attn(q, k, v, w["tiles"], batch, seq_len)
            w_o = w["wo"]
        x = _out_proj(a, w_o, x)
        x = _ffn(x, p["norm_g"][i, 2].reshape(1, -1), *p["ffn"][i][1], p["final_g"], i == DEPTH - 1)
    return x.reshape(batch, seq_len, D_MODEL)


def kernel(x_prompt, x_sample, norm_g, final_g, ffn_w_gate, ffn_w_up, ffn_w_down, rel_bias_table, mla_w_dq, mla_g_q, mla_w_uq, mla_w_dkv, mla_g_kv, mla_w_uk, mla_w_uv, mla_w_o, diff_w_q, diff_w_k, diff_w_v, diff_lam_q1, diff_lam_k1, diff_lam_q2, diff_lam_k2, diff_g_sub, diff_w_o, na_w_qkv, na_rpb, na_w_o):
    max_len = max(x_prompt.shape[1], x_sample.shape[1])
    hd = NA_HEADS * NA_HEAD_DIM
    p = {
        "norm_g": norm_g,
        "final_g": final_g.reshape(1, -1),
        "ffn": [[_ffn_weights(ffn_w_gate[i, s], ffn_w_up[i, s], ffn_w_down[i, s]) for s in range(2)]
                for i in range(DEPTH)],
        "t5": _t5_tiles(rel_bias_table),
        "mla": [dict(_mla_weights(mla_w_dq[j], mla_g_q[j], mla_w_uq[j], mla_w_dkv[j], mla_g_kv[j],
                                  mla_w_uk[j], mla_w_uv[j], max_len), wo=mla_w_o[j].astype(BF16))
                for j in range(mla_w_dq.shape[0])],
        "diff": [{
            "wq": diff_w_q[j].astype(BF16), "wk": diff_w_k[j].astype(BF16), "wvt": diff_w_v[j].T.astype(BF16),
            "lam": tuple(v[j].reshape(1, -1) for v in (diff_lam_q1, diff_lam_k1, diff_lam_q2, diff_lam_k2)),
            "gsub": diff_g_sub[j].reshape(1, -1), "wo": diff_w_o[j].astype(BF16),
        } for j in range(diff_w_q.shape[0])],
        "na": [{
            "wq": na_w_qkv[j][:, :hd].astype(BF16), "wk": na_w_qkv[j][:, hd:2 * hd].astype(BF16),
            "wv": na_w_qkv[j][:, 2 * hd:].astype(BF16), "tiles": _na_tiles(na_rpb[j]),
            "wo": na_w_o[j].astype(BF16),
        } for j in range(na_w_qkv.shape[0])],
    }
    return (_trunk(x_prompt, p), _trunk(x_sample, p))
```

```python
import functools
import math

import jax
import jax.numpy as jnp
from jax import lax
from jax.experimental import pallas as pl
from jax.experimental.pallas import tpu as pltpu

F32 = jnp.float32
BF16 = jnp.bfloat16

D_MODEL = 1024
DEPTH = 4
N_MIXERS = 3
GRID_W = 64
RMS_EPS = 1e-6
D_FF = 2816

REL_BUCKETS = 32
N_BIAS_HEADS = 16

MLA_HEADS = 16
MLA_Q_LORA = 512
MLA_KV_LORA = 256
MLA_NOPE = 64
MLA_ROPE = 32
MLA_V = 64
ROPE_THETA = 10000.0

DIFF_HEADS = 8
DIFF_QK = 64
DIFF_V = 2 * DIFF_QK

NA_HEADS = 16
NA_HEAD_DIM = 64
NA_KR = 8
NA_KC = 16

LANES = 128
VMEM_LIMIT_BYTES = 56 * 1024 * 1024

TOK_TILE = 512
FF_CHUNK = 256
ATT_TQ = 512
ATT_TK = TOK_TILE
BIAS_TILE = LANES
PAIR_W = 2 * NA_HEAD_DIM
NA_BLOCK_ROWS = 4
NA_SLAB_ROWS = NA_BLOCK_ROWS + NA_KR
NA_BLOCK_Q = NA_BLOCK_ROWS * GRID_W
NA_SLAB_Q = NA_SLAB_ROWS * GRID_W
MASK_VALUE = -1e30
ONES_ROWS = 16
LOG2E = math.log2(math.e)

T5_THRESHOLDS = (12, 16, 23, 32, 46, 64, 91)


def _params(*semantics):
    return pltpu.CompilerParams(dimension_semantics=semantics, vmem_limit_bytes=VMEM_LIMIT_BYTES)


def _resident(shape):
    zeros = (0,) * len(shape)
    return pl.BlockSpec(shape, lambda *_: zeros, pipeline_mode=pl.Buffered(1))


def _rms(x, g):
    return x * lax.rsqrt(jnp.mean(x * x, axis=-1, keepdims=True) + RMS_EPS) * g


def _dot(a, b):
    return jnp.dot(a, b, preferred_element_type=F32)


def _dot_nt(a, b):
    return lax.dot_general(a, b, (((1,), (1,)), ((), ())), preferred_element_type=F32)


def _ffn_kernel(*refs, add_mixer, apply_final):
    if add_mixer:
        x_ref, a_ref, wo_ref, g_ref, wg_ref, wu_ref, wd_ref, fg_ref, o_ref = refs
        x = x_ref[...] + _dot(a_ref[...], wo_ref[...])
    else:
        x_ref, g_ref, wg_ref, wu_ref, wd_ref, fg_ref, o_ref = refs
        x = x_ref[...]
    h = _rms(x, g_ref[...]).astype(BF16)
    acc = jnp.zeros(x.shape, F32)
    for c in range(D_FF // FF_CHUNK):
        gate = _dot(h, wg_ref[c])
        up = _dot(h, wu_ref[c])
        act = (gate / (1.0 + jnp.exp(-gate))) * up
        acc = acc + _dot(act.astype(BF16), wd_ref[c])
    y = x + 0.5 * acc
    if apply_final:
        y = _rms(y, fg_ref[...])
    o_ref[...] = y


def _ffn(x, mixer, g, wg, wu, wd, final_g, apply_final):
    t = x.shape[0]
    nc = D_FF // FF_CHUNK
    tok = pl.BlockSpec((TOK_TILE, D_MODEL), lambda i: (i, 0))
    mixer_specs = [] if mixer is None else [tok, _resident((D_MODEL, D_MODEL))]
    return pl.pallas_call(
        functools.partial(_ffn_kernel, add_mixer=mixer is not None, apply_final=apply_final),
        out_shape=jax.ShapeDtypeStruct(x.shape, F32),
        grid=(t // TOK_TILE,),
        in_specs=[tok] + mixer_specs + [
            _resident((1, D_MODEL)),
            _resident((nc, D_MODEL, FF_CHUNK)),
            _resident((nc, D_MODEL, FF_CHUNK)),
            _resident((nc, FF_CHUNK, D_MODEL)),
            _resident((1, D_MODEL)),
        ],
        out_specs=tok,
        compiler_params=_params("parallel"),
        name="ffn_mixer" if mixer is not None else "ffn",
    )(x, *(() if mixer is None else mixer), g, wg, wu, wd, final_g)


def _mla_proj_kernel(x_ref, g_ref, wdq_ref, gq_ref, wuq_ref, wuqr_ref, wckv_ref, gkv_ref,
                     wkr_ref, wkrr_ref, wuk_ref, wuvt_ref, cos_ref, sin_ref,
                     q_ref, k_ref, vt_ref):
    hn = _rms(x_ref[...], g_ref[...]).astype(BF16)
    cos = cos_ref[...]
    sin = sin_ref[...]
    scale = (MLA_NOPE + MLA_ROPE) ** -0.5 * LOG2E

    cq = _rms(_dot(hn, wdq_ref[...]), gq_ref[...]).astype(BF16)
    qa = _dot(cq, wuq_ref[...])
    qb = _dot(cq, wuqr_ref[...])
    for h in range(MLA_HEADS):
        sl = slice(h * LANES, (h + 1) * LANES)
        q_ref[:, sl] = ((qa[:, sl] * cos + qb[:, sl] * sin) * scale).astype(BF16)

    ckv = _rms(_dot(hn, wckv_ref[...]), gkv_ref[...]).astype(BF16)
    k_rope = _dot(hn, wkr_ref[...]) * cos + _dot(hn, wkrr_ref[...]) * sin
    k_nope = _dot(ckv, wuk_ref[...])
    for h in range(MLA_HEADS):
        sl = slice(h * LANES, (h + 1) * LANES)
        k_ref[:, sl] = (k_nope[:, sl] + k_rope).astype(BF16)

    vt_ref[0] = _dot_nt(wuvt_ref[...], ckv).astype(BF16)


def _mla_proj(x, seq_len, g, w):
    t = x.shape[0]
    hw = MLA_HEADS * LANES
    pos_blocks = seq_len // TOK_TILE
    tok = lambda width: pl.BlockSpec((TOK_TILE, width), lambda i: (i, 0))
    pos = pl.BlockSpec((TOK_TILE, LANES), lambda i: (i % pos_blocks, 0))
    return pl.pallas_call(
        _mla_proj_kernel,
        out_shape=(
            jax.ShapeDtypeStruct((t, hw), BF16),
            jax.ShapeDtypeStruct((t, hw), BF16),
            jax.ShapeDtypeStruct((t // TOK_TILE, MLA_HEADS * MLA_V, TOK_TILE), BF16),
        ),
        grid=(t // TOK_TILE,),
        in_specs=[
            tok(D_MODEL),
            _resident((1, D_MODEL)),
            _resident((D_MODEL, MLA_Q_LORA)),
            _resident((1, MLA_Q_LORA)),
            _resident((MLA_Q_LORA, hw)),
            _resident((MLA_Q_LORA, hw)),
            _resident((D_MODEL, MLA_KV_LORA)),
            _resident((1, MLA_KV_LORA)),
            _resident((D_MODEL, LANES)),
            _resident((D_MODEL, LANES)),
            _resident((MLA_KV_LORA, hw)),
            _resident((MLA_HEADS * MLA_V, MLA_KV_LORA)),
            pos,
            pos,
        ],
        out_specs=(
            tok(hw),
            tok(hw),
            pl.BlockSpec((1, MLA_HEADS * MLA_V, TOK_TILE), lambda i: (i, 0, 0)),
        ),
        compiler_params=_params("parallel"),
        name="mla_proj",
    )(x, g, w["wdq"], w["gq"], w["wuq"], w["wuq_rot"], w["wckv"], w["gkv"],
      w["wkr"], w["wkr_rot"], w["wuk"], w["wuvt"], w["cos"], w["sin"])


def _qkv_proj_kernel(x_ref, g_ref, wq_ref, wk_ref, wvt_ref, q_ref, k_ref, vt_ref, *, q_scale):
    hn = _rms(x_ref[...], g_ref[...]).astype(BF16)
    q_ref[...] = (_dot(hn, wq_ref[...]) * q_scale).astype(BF16)
    k_ref[...] = _dot(hn, wk_ref[...]).astype(BF16)
    vt = _dot_nt(wvt_ref[...], hn).astype(BF16)
    chunk = vt_ref.shape[-1]
    for c in range(vt_ref.shape[0]):
        vt_ref[c] = vt[:, c * chunk:(c + 1) * chunk]


def _qkv_proj(x, g, wq, wk, wvt, q_scale, v_chunk):
    t = x.shape[0]
    per_tile = TOK_TILE // v_chunk
    tok = pl.BlockSpec((TOK_TILE, D_MODEL), lambda i: (i, 0))
    return pl.pallas_call(
        functools.partial(_qkv_proj_kernel, q_scale=q_scale),
        out_shape=(jax.ShapeDtypeStruct((t, D_MODEL), BF16), jax.ShapeDtypeStruct((t, D_MODEL), BF16),
                   jax.ShapeDtypeStruct((t // v_chunk, D_MODEL, v_chunk), BF16)),
        grid=(t // TOK_TILE,),
        in_specs=[tok, _resident((1, D_MODEL)), _resident((D_MODEL, D_MODEL)),
                  _resident((D_MODEL, D_MODEL)), _resident((D_MODEL, D_MODEL))],
        out_specs=(tok, tok, pl.BlockSpec((per_tile, D_MODEL, v_chunk), lambda i: (i, 0, 0))),
        compiler_params=_params("parallel"),
        name="qkv_proj",
    )(x, g, wq, wk, wvt)


def _t5_tiles_kernel(tab_ref, o_ref):
    head = pl.program_id(0)
    kk = lax.broadcasted_iota(jnp.int32, (BIAS_TILE, BIAS_TILE), 0)
    qq = lax.broadcasted_iota(jnp.int32, (BIAS_TILE, BIAS_TILE), 1)
    half = REL_BUCKETS // 2
    max_exact = half // 2
    for d in range(5):
        rel = (d - 2) * BIAS_TILE + kk - qq
        n = jnp.where(rel < 0, -rel, rel)
        big = jnp.full(rel.shape, max_exact, jnp.int32)
        for thr in T5_THRESHOLDS:
            big = big + jnp.where(n >= thr, 1, 0)
        bucket = jnp.where(rel > 0, half, 0) + jnp.where(n < max_exact, n, big)
        out = jnp.zeros(rel.shape, F32)
        for b in range(REL_BUCKETS):
            out = jnp.where(bucket == b, tab_ref[b, head], out)
        o_ref[d, 0] = out * LOG2E


def _t5_tiles(table):
    return pl.pallas_call(
        _t5_tiles_kernel,
        out_shape=jax.ShapeDtypeStruct((5, N_BIAS_HEADS, BIAS_TILE, BIAS_TILE), F32),
        grid=(N_BIAS_HEADS,),
        in_specs=[pl.BlockSpec(memory_space=pltpu.SMEM)],
        out_specs=pl.BlockSpec((5, 1, BIAS_TILE, BIAS_TILE), lambda h: (0, h, 0, 0)),
        compiler_params=_params("parallel"),
        name="t5_tiles",
    )(table)


def _na_block_delta(block_type, key_row, query_row):
    if block_type == 0:
        return key_row - query_row if key_row < NA_KR else None
    if block_type == 1:
        delta = key_row - query_row - NA_KR // 2
        return delta if -(NA_KR // 2) <= delta < NA_KR // 2 else None
    return key_row - query_row - NA_KR if key_row >= NA_SLAB_ROWS - NA_KR else None


def _na_bias_kernel(rpb_ref, o_ref):
    head = pl.program_id(0)
    n_col = 2 * NA_KC - 1
    shape = (GRID_W, NA_BLOCK_Q)
    kc = lax.broadcasted_iota(jnp.int32, shape, 0)
    col = lax.broadcasted_iota(jnp.int32, shape, 1)
    c = col & (GRID_W - 1)
    block_row = col >> int(math.log2(GRID_W))
    start = jnp.clip(c - NA_KC // 2, 0, GRID_W - NA_KC)
    valid = (kc >= start) & (kc < start + NA_KC)
    col_off = kc - c + (NA_KC - 1)
    masked = jnp.full(shape, MASK_VALUE, F32)
    per_row_offset = []
    for ro in range(2 * NA_KR - 1):
        out = masked
        for co in range(n_col):
            out = jnp.where(valid & (col_off == co), rpb_ref[head, ro * n_col + co] * LOG2E, out)
        per_row_offset.append(out)
    for block_type in range(3):
        for key_row in range(NA_SLAB_ROWS):
            piece = masked
            for query_row in range(NA_BLOCK_ROWS):
                delta = _na_block_delta(block_type, key_row, query_row)
                if delta is not None:
                    piece = jnp.where(block_row == query_row, per_row_offset[delta + NA_KR - 1], piece)
            o_ref[0, block_type, key_row * GRID_W:(key_row + 1) * GRID_W, :] = piece


def _na_bias(rpb):
    n_row = 2 * NA_KR - 1
    n_col = 2 * NA_KC - 1
    return pl.pallas_call(
        _na_bias_kernel,
        out_shape=jax.ShapeDtypeStruct((NA_HEADS, 3, NA_SLAB_Q, NA_BLOCK_Q), F32),
        grid=(NA_HEADS,),
        in_specs=[pl.BlockSpec(memory_space=pltpu.SMEM)],
        out_specs=pl.BlockSpec((1, 3, NA_SLAB_Q, NA_BLOCK_Q), lambda h: (h, 0, 0, 0)),
        compiler_params=_params("parallel"),
        name="na_bias",
    )(rpb.reshape(NA_HEADS, n_row * n_col))


def _dense_attn_kernel(q_ref, k_ref, vt_ref, bias_ref, lq1_ref, lk1_ref, lq2_ref, lk2_ref, gsub_ref,
                       o_ref, qm_ref, sa_ref, sb_ref, mba_ref, mbb_ref, m_ref, acc_ref, *, diff, lam_init):
    qi = pl.program_id(2)
    n_kb = vt_ref.shape[0]
    sub = ATT_TK // BIAS_TILE
    dv = DIFF_V if diff else MLA_V

    q = q_ref[...]
    if diff:
        lane = lax.broadcasted_iota(jnp.int32, q.shape, 1)
        zero = jnp.zeros_like(q)
        qm_ref[0] = jnp.where(lane < DIFF_QK, q, zero)
        qm_ref[1] = jnp.where(lane >= DIFF_QK, q, zero)
    else:
        qm_ref[0] = q[:, :LANES]
        qm_ref[1] = q[:, LANES:]
    m_ref[...] = jnp.full(m_ref.shape, MASK_VALUE, F32)
    acc_ref[...] = jnp.zeros(acc_ref.shape, F32)
    ones = jnp.ones((ONES_ROWS, ATT_TK), BF16)

    def scores(j, buf, near):
        s_ref, mb_ref = buf
        kb = k_ref[pl.ds(pl.multiple_of(j * ATT_TK, ATT_TK), ATT_TK), :]
        base = (j - qi) * sub + 2
        for t in range(2):
            kt = kb if diff else kb[:, t * LANES:(t + 1) * LANES]
            s = _dot_nt(kt, qm_ref[t])
            if near:
                s = s + jnp.concatenate(
                    [jnp.concatenate(
                        [bias_ref[jnp.clip(base + a - b, 0, 4), t] for b in range(ATT_TQ // BIAS_TILE)], axis=1)
                     for a in range(sub)], axis=0)
            s_ref[t] = s
            mb_ref[t] = jnp.max(s, axis=0, keepdims=True)

    def far_bias(j):
        left = j < qi
        return [jnp.where(left, bias_ref[0, t, 0:1, 0:1], bias_ref[4, t, 0:1, 0:1]) for t in range(2)]

    def update(j, buf, bias_const):
        s_ref, mb_ref = buf
        vb = vt_ref[j]
        for t in range(2):
            m_prev = m_ref[t]
            c = 0.0 if bias_const is None else bias_const[t]
            m_new = jnp.maximum(m_prev, mb_ref[t] + c)
            alpha = jnp.exp2(m_prev - m_new)
            p = jnp.exp2(s_ref[t] - (m_new - c)).astype(BF16)
            vv = vb if diff else vb[t * MLA_V:(t + 1) * MLA_V]
            acc_ref[t] = acc_ref[t] * alpha + _dot(jnp.concatenate([vv, ones], axis=0), p)
            m_ref[t] = m_new

    buf_a = (sa_ref, mba_ref)
    buf_b = (sb_ref, mbb_ref)
    n_far = n_kb - 3
    w0 = jnp.clip(qi - 1, 0, n_far)
    far_block = lambda i: jnp.where(i < w0, i, i + 3)

    scores(w0, buf_a, True)
    scores(w0 + 1, buf_b, True)
    update(w0, buf_a, None)
    scores(w0 + 2, buf_a, True)
    update(w0 + 1, buf_b, None)
    scores(far_block(0), buf_b, False)
    update(w0 + 2, buf_a, None)

    def two_far_blocks(pair, carry):
        j0, j1, j2 = far_block(2 * pair), far_block(2 * pair + 1), far_block(2 * pair + 2)
        scores(j1, buf_a, False)
        update(j0, buf_b, far_bias(j0))
        scores(j2, buf_b, False)
        update(j1, buf_a, far_bias(j1))
        return carry

    lax.fori_loop(0, (n_far - 1) // 2, two_far_blocks, 0)
    j_last = far_block(n_far - 1)
    update(j_last, buf_b, far_bias(j_last))

    a0 = acc_ref[0]
    a1 = acc_ref[1]
    o0 = a0[:dv] / a0[dv:dv + 1]
    o1 = a1[:dv] / a1[dv:dv + 1]
    if diff:
        lam = (jnp.exp(jnp.sum(lq1_ref[...] * lk1_ref[...], axis=-1, keepdims=True))
               - jnp.exp(jnp.sum(lq2_ref[...] * lk2_ref[...], axis=-1, keepdims=True)) + lam_init)
        o = (o0 - lam * o1).T
        o = _rms(o, gsub_ref[...]) * (1.0 - lam_init)
    else:
        o = jnp.concatenate([o0, o1], axis=0).T
    o_ref[...] = o.astype(BF16)


def _dense_attn(q, k, vt, bias, lam_vecs, gsub, batch, seq_len, diff, lam_init):
    pw = PAIR_W if diff else 2 * LANES
    dv = DIFF_V if diff else MLA_V
    nq = seq_len // ATT_TQ
    nk = seq_len // ATT_TK
    assert nk % 2 == 0 and nk >= 4, "three near blocks, then far blocks two per loop trip after the first"
    small = pl.BlockSpec((1, DIFF_QK), lambda b, h, i: (0, 0))
    return pl.pallas_call(
        functools.partial(_dense_attn_kernel, diff=diff, lam_init=lam_init),
        out_shape=jax.ShapeDtypeStruct((batch * seq_len, D_MODEL), BF16),
        grid=(batch, N_BIAS_HEADS // 2, nq),
        in_specs=[
            pl.BlockSpec((ATT_TQ, pw), lambda b, h, i: (b * nq + i, h)),
            pl.BlockSpec((seq_len, pw), lambda b, h, i: (b, h)),
            pl.BlockSpec((nk, PAIR_W, ATT_TK), lambda b, h, i: (b, h, 0)),
            pl.BlockSpec((5, 2, BIAS_TILE, BIAS_TILE), lambda b, h, i: (0, h, 0, 0)),
            small, small, small, small,
            pl.BlockSpec((1, DIFF_V), lambda b, h, i: (0, 0)),
        ],
        out_specs=pl.BlockSpec((ATT_TQ, PAIR_W), lambda b, h, i: (b * nq + i, h)),
        scratch_shapes=[
            pltpu.VMEM((2, ATT_TQ, LANES), BF16),
            pltpu.VMEM((2, ATT_TK, ATT_TQ), F32),
            pltpu.VMEM((2, ATT_TK, ATT_TQ), F32),
            pltpu.VMEM((2, 1, ATT_TQ), F32),
            pltpu.VMEM((2, 1, ATT_TQ), F32),
            pltpu.VMEM((2, 1, ATT_TQ), F32),
            pltpu.VMEM((2, dv + ONES_ROWS, ATT_TQ), F32),
        ],
        compiler_params=_params("parallel", "parallel", "parallel"),
        name="diff_attn" if diff else "mla_attn",
    )(q, k, vt, bias, *lam_vecs, gsub)


def _na_attn_kernel(q_ref, k_ref, vt_ref, bias_ref, o_ref, sa_ref, sb_ref, mba_ref, mbb_ref):
    n_blocks = q_ref.shape[0] // NA_BLOCK_Q
    chunks = NA_SLAB_Q // NA_BLOCK_Q
    lane = lax.broadcasted_iota(jnp.int32, (NA_BLOCK_Q, PAIR_W), 1)
    first = lane < NA_HEAD_DIM
    ones = jnp.ones((ONES_ROWS, NA_SLAB_Q), BF16)
    slab_chunk = lambda g: jnp.clip(g - 1, 0, n_blocks - chunks)

    def scores(g, buf):
        s_ref, mb_ref = buf
        block_type = jnp.where(g == 0, 0, jnp.where(g == n_blocks - 1, 2, 1))
        q = q_ref[pl.ds(pl.multiple_of(g * NA_BLOCK_Q, NA_BLOCK_Q), NA_BLOCK_Q), :]
        ks = k_ref[pl.ds(pl.multiple_of(slab_chunk(g) * NA_BLOCK_Q, NA_BLOCK_Q), NA_SLAB_Q), :]
        zero = jnp.zeros_like(q)
        for t in range(2):
            qt = jnp.where(first if t == 0 else ~first, q, zero)
            s = _dot_nt(ks, qt) + bias_ref[t, block_type]
            s_ref[t] = s
            mb_ref[t] = jnp.max(s, axis=0, keepdims=True)

    def finish(g, buf):
        s_ref, mb_ref = buf
        c0 = slab_chunk(g)
        vts = jnp.concatenate([vt_ref[c0 + i] for i in range(chunks)], axis=1)
        outs = []
        for t in range(2):
            p = jnp.exp2(s_ref[t] - mb_ref[t]).astype(BF16)
            vv = vts[t * NA_HEAD_DIM:(t + 1) * NA_HEAD_DIM]
            acc = _dot(jnp.concatenate([vv, ones], axis=0), p)
            outs.append(acc[:NA_HEAD_DIM] / acc[NA_HEAD_DIM:NA_HEAD_DIM + 1])
        o = jnp.concatenate(outs, axis=0).T
        o_ref[pl.ds(pl.multiple_of(g * NA_BLOCK_Q, NA_BLOCK_Q), NA_BLOCK_Q), :] = o.astype(BF16)

    buf_a = (sa_ref, mba_ref)
    buf_b = (sb_ref, mbb_ref)
    scores(0, buf_a)

    def two_blocks(pair, carry):
        g = 2 * pair
        scores(g + 1, buf_b)
        finish(g, buf_a)
        scores(g + 2, buf_a)
        finish(g + 1, buf_b)
        return carry

    lax.fori_loop(0, n_blocks // 2 - 1, two_blocks, 0)
    scores(n_blocks - 1, buf_b)
    finish(n_blocks - 2, buf_a)
    finish(n_blocks - 1, buf_b)


def _na_attn(q, k, vt, bias, batch, seq_len):
    n_blocks = seq_len // NA_BLOCK_Q
    assert n_blocks % 2 == 0 and n_blocks * NA_BLOCK_ROWS >= NA_SLAB_ROWS
    seq = pl.BlockSpec((seq_len, PAIR_W), lambda h, b: (b, h))
    return pl.pallas_call(
        _na_attn_kernel,
        out_shape=jax.ShapeDtypeStruct((batch * seq_len, D_MODEL), BF16),
        grid=(NA_HEADS // 2, batch),
        in_specs=[seq, seq,
                  pl.BlockSpec((n_blocks, PAIR_W, NA_BLOCK_Q), lambda h, b: (b, h, 0)),
                  pl.BlockSpec((2, 3, NA_SLAB_Q, NA_BLOCK_Q), lambda h, b: (h, 0, 0, 0))],
        out_specs=seq,
        scratch_shapes=[
            pltpu.VMEM((2, NA_SLAB_Q, NA_BLOCK_Q), F32),
            pltpu.VMEM((2, NA_SLAB_Q, NA_BLOCK_Q), F32),
            pltpu.VMEM((2, 1, NA_BLOCK_Q), F32),
            pltpu.VMEM((2, 1, NA_BLOCK_Q), F32),
        ],
        compiler_params=_params("parallel", "parallel"),
        name="na_attn",
    )(q, k, vt, bias)


def _pad_heads(w, heads, dim):
    w = w.reshape(w.shape[0], heads, dim)
    return jnp.pad(w, ((0, 0), (0, 0), (0, LANES - dim))).reshape(w.shape[0], heads * LANES)


def _rot_half_cols(w_rope):
    half = w_rope.shape[-1] // 2
    return jnp.concatenate([-w_rope[..., half:], w_rope[..., :half]], axis=-1)


def _rope_tables(max_len):
    half = MLA_ROPE // 2
    freqs = ROPE_THETA ** (-jnp.arange(half, dtype=F32) / half)
    ang = jnp.arange(max_len, dtype=F32)[:, None] * freqs[None, :]
    cos = jnp.cos(ang)
    sin = jnp.sin(ang)
    pad = jnp.zeros((max_len, LANES - MLA_NOPE - MLA_ROPE), F32)
    cos_t = jnp.concatenate([jnp.ones((max_len, MLA_NOPE), F32), cos, cos, pad], axis=-1)
    sin_t = jnp.concatenate([jnp.zeros((max_len, MLA_NOPE), F32), sin, sin, pad], axis=-1)
    return cos_t, sin_t


def _mla_weights(w_dq, g_q, w_uq, w_dkv, g_kv, w_uk, w_uv, max_len):
    d_qk = MLA_NOPE + MLA_ROPE
    uq = w_uq.reshape(MLA_Q_LORA, MLA_HEADS, d_qk)
    uq_rot = jnp.concatenate(
        [jnp.zeros((MLA_Q_LORA, MLA_HEADS, MLA_NOPE), F32), _rot_half_cols(uq[..., MLA_NOPE:])], axis=-1)
    w_kr = w_dkv[:, MLA_KV_LORA:]
    place = lambda w: jnp.pad(w, ((0, 0), (MLA_NOPE, LANES - d_qk)))
    cos_t, sin_t = _rope_tables(max_len)
    return {
        "wdq": w_dq.astype(BF16),
        "gq": g_q.reshape(1, -1),
        "wuq": _pad_heads(uq.reshape(MLA_Q_LORA, -1), MLA_HEADS, d_qk).astype(BF16),
        "wuq_rot": _pad_heads(uq_rot.reshape(MLA_Q_LORA, -1), MLA_HEADS, d_qk).astype(BF16),
        "wckv": w_dkv[:, :MLA_KV_LORA].astype(BF16),
        "gkv": g_kv.reshape(1, -1),
        "wkr": place(w_kr).astype(BF16),
        "wkr_rot": place(_rot_half_cols(w_kr)).astype(BF16),
        "wuk": _pad_heads(w_uk, MLA_HEADS, MLA_NOPE).astype(BF16),
        "wuvt": w_uv.T.astype(BF16),
        "cos": cos_t,
        "sin": sin_t,
    }


def _ffn_weights(w_gate, w_up, w_down):
    nc = D_FF // FF_CHUNK
    cols = lambda w: w.reshape(D_MODEL, nc, FF_CHUNK).transpose(1, 0, 2).astype(BF16)
    return cols(w_gate), cols(w_up), w_down.reshape(nc, FF_CHUNK, D_MODEL).astype(BF16)


def _trunk(x, p):
    batch, seq_len, _ = x.shape
    x = x.reshape(batch * seq_len, D_MODEL)
    dummy_vecs = (jnp.zeros((1, DIFF_QK), F32),) * 4
    dummy_g = jnp.ones((1, DIFF_V), F32)
    for i in range(DEPTH):
        x = _ffn(x, None, p["norm_g"][i, 0].reshape(1, -1), *p["ffn"][i][0], p["final_g"], False)
        g_mix = p["norm_g"][i, 1].reshape(1, -1)
        m, j = i % N_MIXERS, i // N_MIXERS
        if m == 0:
            w = p["mla"][j]
            q, k, vt = _mla_proj(x, seq_len, g_mix, w)
            a = _dense_attn(q, k, vt, p["t5"], dummy_vecs, dummy_g, batch, seq_len, False, 0.0)
        elif m == 1:
            w = p["diff"][j]
            lam_init = 0.8 - 0.6 * math.exp(-0.3 * i)
            q, k, vt = _qkv_proj(x, g_mix, w["wq"], w["wk"], w["wvt"], DIFF_QK ** -0.5 * LOG2E, ATT_TK)
            a = _dense_attn(q, k, vt, p["t5"], w["lam"], w["gsub"], batch, seq_len, True, lam_init)
        else:
            w = p["na"][j]
            q, k, vt = _qkv_proj(x, g_mix, w["wq"], w["wk"], w["wvt"], NA_HEAD_DIM ** -0.5 * LOG2E, NA_BLOCK_Q)
            a = _na_attn(q, k, vt, w["bias"], batch, seq_len)
        x = _ffn(x, (a, w["wo"]), p["norm_g"][i, 2].reshape(1, -1), *p["ffn"][i][1], p["final_g"], i == DEPTH - 1)
    return x.reshape(batch, seq_len, D_MODEL)


def kernel(x_prompt, x_sample, norm_g, final_g, ffn_w_gate, ffn_w_up, ffn_w_down, rel_bias_table, mla_w_dq, mla_g_q, mla_w_uq, mla_w_dkv, mla_g_kv, mla_w_uk, mla_w_uv, mla_w_o, diff_w_q, diff_w_k, diff_w_v, diff_lam_q1, diff_lam_k1, diff_lam_q2, diff_lam_k2, diff_g_sub, diff_w_o, na_w_qkv, na_rpb, na_w_o):
    max_len = max(x_prompt.shape[1], x_sample.shape[1])
    hd = NA_HEADS * NA_HEAD_DIM
    p = {
        "norm_g": norm_g,
        "final_g": final_g.reshape(1, -1),
        "ffn": [[_ffn_weights(ffn_w_gate[i, s], ffn_w_up[i, s], ffn_w_down[i, s]) for s in range(2)]
                for i in range(DEPTH)],
        "t5": _t5_tiles(rel_bias_table),
        "mla": [dict(_mla_weights(mla_w_dq[j], mla_g_q[j], mla_w_uq[j], mla_w_dkv[j], mla_g_kv[j],
                                  mla_w_uk[j], mla_w_uv[j], max_len), wo=mla_w_o[j].astype(BF16))
                for j in range(mla_w_dq.shape[0])],
        "diff": [{
            "wq": diff_w_q[j].astype(BF16), "wk": diff_w_k[j].astype(BF16), "wvt": diff_w_v[j].T.astype(BF16),
            "lam": tuple(v[j].reshape(1, -1) for v in (diff_lam_q1, diff_lam_k1, diff_lam_q2, diff_lam_k2)),
            "gsub": diff_g_sub[j].reshape(1, -1), "wo": diff_w_o[j].astype(BF16),
        } for j in range(diff_w_q.shape[0])],
        "na": [{
            "wq": na_w_qkv[j][:, :hd].astype(BF16), "wk": na_w_qkv[j][:, hd:2 * hd].astype(BF16),
            "wvt": na_w_qkv[j][:, 2 * hd:].T.astype(BF16), "bias": _na_bias(na_rpb[j]),
            "wo": na_w_o[j].astype(BF16),
        } for j in range(na_w_qkv.shape[0])],
    }
    return (_trunk(x_prompt, p), _trunk(x_sample, p))
```

```python
import functools
import math

import jax
import jax.numpy as jnp
from jax import lax
from jax.experimental import pallas as pl
from jax.experimental.pallas import tpu as pltpu

F32 = jnp.float32
BF16 = jnp.bfloat16

D_MODEL = 1024
DEPTH = 4
N_MIXERS = 3
GRID_W = 64
RMS_EPS = 1e-6
D_FF = 2816

REL_BUCKETS = 32
N_BIAS_HEADS = 16

MLA_HEADS = 16
MLA_Q_LORA = 512
MLA_KV_LORA = 256
MLA_NOPE = 64
MLA_ROPE = 32
MLA_V = 64
ROPE_THETA = 10000.0

DIFF_HEADS = 8
DIFF_QK = 64
DIFF_V = 2 * DIFF_QK

NA_HEADS = 16
NA_HEAD_DIM = 64
NA_KR = 8
NA_KC = 16

LANES = 128
VMEM_LIMIT_BYTES = 56 * 1024 * 1024

TOK_TILE = 512
FF_CHUNK = 256
ATT_TQ = 1024
VEC_ROWS = 32
ATT_TK = TOK_TILE
BIAS_TILE = LANES
PAIR_W = 2 * NA_HEAD_DIM
NA_BLOCK_ROWS = 4
NA_SLAB_ROWS = NA_BLOCK_ROWS + NA_KR
NA_BLOCK_Q = NA_BLOCK_ROWS * GRID_W
NA_SLAB_Q = NA_SLAB_ROWS * GRID_W
MASK_VALUE = -1e30
ONES_ROWS = 16
LOG2E = math.log2(math.e)

T5_THRESHOLDS = (12, 16, 23, 32, 46, 64, 91)


def _params(*semantics):
    return pltpu.CompilerParams(dimension_semantics=semantics, vmem_limit_bytes=VMEM_LIMIT_BYTES)


def _resident(shape):
    zeros = (0,) * len(shape)
    return pl.BlockSpec(shape, lambda *_: zeros, pipeline_mode=pl.Buffered(1))


def _rms(x, g):
    return x * lax.rsqrt(jnp.mean(x * x, axis=-1, keepdims=True) + RMS_EPS) * g


def _dot(a, b):
    return jnp.dot(a, b, preferred_element_type=F32)


def _dot_nt(a, b):
    return lax.dot_general(a, b, (((1,), (1,)), ((), ())), preferred_element_type=F32)


def _ffn_kernel(*refs, add_mixer, apply_final):
    if add_mixer:
        x_ref, a_ref, wo_ref, g_ref, wg_ref, wu_ref, wd_ref, fg_ref, o_ref = refs
        x = x_ref[...] + _dot(a_ref[...], wo_ref[...])
    else:
        x_ref, g_ref, wg_ref, wu_ref, wd_ref, fg_ref, o_ref = refs
        x = x_ref[...]
    h = _rms(x, g_ref[...]).astype(BF16)
    acc = jnp.zeros(x.shape, F32)
    for c in range(D_FF // FF_CHUNK):
        sl = slice(c * FF_CHUNK, (c + 1) * FF_CHUNK)
        gate = _dot(h, wg_ref[:, sl])
        up = _dot(h, wu_ref[:, sl])
        act = (gate / (1.0 + jnp.exp(-gate))) * up
        acc = acc + _dot(act.astype(BF16), wd_ref[sl, :])
    y = x + 0.5 * acc
    if apply_final:
        y = _rms(y, fg_ref[...])
    o_ref[...] = y


def _ffn(x, mixer, g, wg, wu, wd, final_g, apply_final):
    t = x.shape[0]
    tok = pl.BlockSpec((TOK_TILE, D_MODEL), lambda i: (i, 0))
    mixer_specs = [] if mixer is None else [tok, _resident((D_MODEL, D_MODEL))]
    return pl.pallas_call(
        functools.partial(_ffn_kernel, add_mixer=mixer is not None, apply_final=apply_final),
        out_shape=jax.ShapeDtypeStruct(x.shape, F32),
        grid=(t // TOK_TILE,),
        in_specs=[tok] + mixer_specs + [
            _resident((1, D_MODEL)),
            _resident((D_MODEL, D_FF)),
            _resident((D_MODEL, D_FF)),
            _resident((D_FF, D_MODEL)),
            _resident((1, D_MODEL)),
        ],
        out_specs=tok,
        compiler_params=_params("parallel"),
        name="ffn_mixer" if mixer is not None else "ffn",
    )(x, *(() if mixer is None else mixer), g, wg, wu, wd, final_g)


def _mla_proj_kernel(x_ref, g_ref, wdq_ref, gq_ref, wuq_ref, wuqr_ref, wckv_ref, gkv_ref,
                     wkr_ref, wkrr_ref, wuk_ref, wuvt_ref, cos_ref, sin_ref,
                     q_ref, k_ref, vt_ref):
    hn = _rms(x_ref[...], g_ref[...]).astype(BF16)
    cos = cos_ref[...]
    sin = sin_ref[...]
    scale = (MLA_NOPE + MLA_ROPE) ** -0.5 * LOG2E

    cq = _rms(_dot(hn, wdq_ref[...]), gq_ref[...]).astype(BF16)
    qa = _dot(cq, wuq_ref[...])
    qb = _dot(cq, wuqr_ref[...])
    for h in range(MLA_HEADS):
        sl = slice(h * LANES, (h + 1) * LANES)
        q_ref[:, sl] = ((qa[:, sl] * cos + qb[:, sl] * sin) * scale).astype(BF16)

    ckv = _rms(_dot(hn, wckv_ref[...]), gkv_ref[...]).astype(BF16)
    k_rope = _dot(hn, wkr_ref[...]) * cos + _dot(hn, wkrr_ref[...]) * sin
    k_nope = _dot(ckv, wuk_ref[...])
    for h in range(MLA_HEADS):
        sl = slice(h * LANES, (h + 1) * LANES)
        k_ref[:, sl] = (k_nope[:, sl] + k_rope).astype(BF16)

    vt_ref[0] = _dot_nt(wuvt_ref[...], ckv).astype(BF16)


def _mla_proj(x, seq_len, g, w):
    t = x.shape[0]
    hw = MLA_HEADS * LANES
    pos_blocks = seq_len // TOK_TILE
    tok = lambda width: pl.BlockSpec((TOK_TILE, width), lambda i: (i, 0))
    pos = pl.BlockSpec((TOK_TILE, LANES), lambda i: (i % pos_blocks, 0))
    return pl.pallas_call(
        _mla_proj_kernel,
        out_shape=(
            jax.ShapeDtypeStruct((t, hw), BF16),
            jax.ShapeDtypeStruct((t, hw), BF16),
            jax.ShapeDtypeStruct((t // TOK_TILE, MLA_HEADS * MLA_V, TOK_TILE), BF16),
        ),
        grid=(t // TOK_TILE,),
        in_specs=[
            tok(D_MODEL),
            _resident((1, D_MODEL)),
            _resident((D_MODEL, MLA_Q_LORA)),
            _resident((1, MLA_Q_LORA)),
            _resident((MLA_Q_LORA, hw)),
            _resident((MLA_Q_LORA, hw)),
            _resident((D_MODEL, MLA_KV_LORA)),
            _resident((1, MLA_KV_LORA)),
            _resident((D_MODEL, LANES)),
            _resident((D_MODEL, LANES)),
            _resident((MLA_KV_LORA, hw)),
            _resident((MLA_HEADS * MLA_V, MLA_KV_LORA)),
            pos,
            pos,
        ],
        out_specs=(
            tok(hw),
            tok(hw),
            pl.BlockSpec((1, MLA_HEADS * MLA_V, TOK_TILE), lambda i: (i, 0, 0)),
        ),
        compiler_params=_params("parallel"),
        name="mla_proj",
    )(x, g, w["wdq"], w["gq"], w["wuq"], w["wuq_rot"], w["wckv"], w["gkv"],
      w["wkr"], w["wkr_rot"], w["wuk"], w["wuvt"], w["cos"], w["sin"])


def _qkv_proj_kernel(x_ref, g_ref, wq_ref, wk_ref, wvt_ref, q_ref, k_ref, vt_ref, *, q_scale):
    hn = _rms(x_ref[...], g_ref[...]).astype(BF16)
    q_ref[...] = (_dot(hn, wq_ref[...]) * q_scale).astype(BF16)
    k_ref[...] = _dot(hn, wk_ref[...]).astype(BF16)
    vt = _dot_nt(wvt_ref[...], hn).astype(BF16)
    chunk = vt_ref.shape[-1]
    for c in range(vt_ref.shape[0]):
        vt_ref[c] = vt[:, c * chunk:(c + 1) * chunk]


def _qkv_proj(x, g, wq, wk, wvt, q_scale, v_chunk):
    t = x.shape[0]
    per_tile = TOK_TILE // v_chunk
    tok = pl.BlockSpec((TOK_TILE, D_MODEL), lambda i: (i, 0))
    return pl.pallas_call(
        functools.partial(_qkv_proj_kernel, q_scale=q_scale),
        out_shape=(jax.ShapeDtypeStruct((t, D_MODEL), BF16), jax.ShapeDtypeStruct((t, D_MODEL), BF16),
                   jax.ShapeDtypeStruct((t // v_chunk, D_MODEL, v_chunk), BF16)),
        grid=(t // TOK_TILE,),
        in_specs=[tok, _resident((1, D_MODEL)), _resident((D_MODEL, D_MODEL)),
                  _resident((D_MODEL, D_MODEL)), _resident((D_MODEL, D_MODEL))],
        out_specs=(tok, tok, pl.BlockSpec((per_tile, D_MODEL, v_chunk), lambda i: (i, 0, 0))),
        compiler_params=_params("parallel"),
        name="qkv_proj",
    )(x, g, wq, wk, wvt)


def _t5_tiles_kernel(tab_ref, o_ref):
    head = pl.program_id(0)
    kk = lax.broadcasted_iota(jnp.int32, (BIAS_TILE, BIAS_TILE), 0)
    qq = lax.broadcasted_iota(jnp.int32, (BIAS_TILE, BIAS_TILE), 1)
    half = REL_BUCKETS // 2
    max_exact = half // 2
    for d in range(5):
        rel = (d - 2) * BIAS_TILE + kk - qq
        n = jnp.where(rel < 0, -rel, rel)
        big = jnp.full(rel.shape, max_exact, jnp.int32)
        for thr in T5_THRESHOLDS:
            big = big + jnp.where(n >= thr, 1, 0)
        bucket = jnp.where(rel > 0, half, 0) + jnp.where(n < max_exact, n, big)
        out = jnp.zeros(rel.shape, F32)
        for b in range(REL_BUCKETS):
            out = jnp.where(bucket == b, tab_ref[b, head], out)
        o_ref[d, 0] = out * LOG2E


def _t5_tiles(table):
    return pl.pallas_call(
        _t5_tiles_kernel,
        out_shape=jax.ShapeDtypeStruct((5, N_BIAS_HEADS, BIAS_TILE, BIAS_TILE), F32),
        grid=(N_BIAS_HEADS,),
        in_specs=[pl.BlockSpec(memory_space=pltpu.SMEM)],
        out_specs=pl.BlockSpec((5, 1, BIAS_TILE, BIAS_TILE), lambda h: (0, h, 0, 0)),
        compiler_params=_params("parallel"),
        name="t5_tiles",
    )(table)


def _na_block_delta(block_type, key_row, query_row):
    if block_type == 0:
        return key_row - query_row if key_row < NA_KR else None
    if block_type == 1:
        delta = key_row - query_row - NA_KR // 2
        return delta if -(NA_KR // 2) <= delta < NA_KR // 2 else None
    return key_row - query_row - NA_KR if key_row >= NA_SLAB_ROWS - NA_KR else None


def _na_bias_kernel(rpb_ref, o_ref):
    head = pl.program_id(0)
    n_col = 2 * NA_KC - 1
    shape = (GRID_W, NA_BLOCK_Q)
    kc = lax.broadcasted_iota(jnp.int32, shape, 0)
    col = lax.broadcasted_iota(jnp.int32, shape, 1)
    c = col & (GRID_W - 1)
    block_row = col >> int(math.log2(GRID_W))
    start = jnp.clip(c - NA_KC // 2, 0, GRID_W - NA_KC)
    valid = (kc >= start) & (kc < start + NA_KC)
    col_off = kc - c + (NA_KC - 1)
    masked = jnp.full(shape, MASK_VALUE, F32)
    per_row_offset = []
    for ro in range(2 * NA_KR - 1):
        out = masked
        for co in range(n_col):
            out = jnp.where(valid & (col_off == co), rpb_ref[head, ro * n_col + co] * LOG2E, out)
        per_row_offset.append(out)
    for block_type in range(3):
        for key_row in range(NA_SLAB_ROWS):
            piece = masked
            for query_row in range(NA_BLOCK_ROWS):
                delta = _na_block_delta(block_type, key_row, query_row)
                if delta is not None:
                    piece = jnp.where(block_row == query_row, per_row_offset[delta + NA_KR - 1], piece)
            o_ref[0, block_type, key_row * GRID_W:(key_row + 1) * GRID_W, :] = piece


def _na_bias(rpb):
    n_row = 2 * NA_KR - 1
    n_col = 2 * NA_KC - 1
    return pl.pallas_call(
        _na_bias_kernel,
        out_shape=jax.ShapeDtypeStruct((NA_HEADS, 3, NA_SLAB_Q, NA_BLOCK_Q), F32),
        grid=(NA_HEADS,),
        in_specs=[pl.BlockSpec(memory_space=pltpu.SMEM)],
        out_specs=pl.BlockSpec((1, 3, NA_SLAB_Q, NA_BLOCK_Q), lambda h: (h, 0, 0, 0)),
        compiler_params=_params("parallel"),
        name="na_bias",
    )(rpb.reshape(NA_HEADS, n_row * n_col))


def _dense_attn_kernel(q_ref, k_ref, vt_ref, bias_ref, vec_ref, o_ref, qm_ref, sa_ref, sb_ref, stat_ref, acc_ref,
                       *, diff, lam_init):
    qi = pl.program_id(2)
    n_kb = vt_ref.shape[0]
    sub_k = ATT_TK // BIAS_TILE
    sub_q = ATT_TQ // BIAS_TILE
    dv = DIFF_V if diff else MLA_V
    run_max, max_a, max_b = 0, 2, 4

    q = q_ref[...]
    if diff:
        lane = lax.broadcasted_iota(jnp.int32, q.shape, 1)
        zero = jnp.zeros_like(q)
        qm_ref[0] = jnp.where(lane < DIFF_QK, q, zero)
        qm_ref[1] = jnp.where(lane >= DIFF_QK, q, zero)
    else:
        qm_ref[0] = q[:, :LANES]
        qm_ref[1] = q[:, LANES:]
    stat_ref[...] = jnp.full(stat_ref.shape, MASK_VALUE, F32)
    acc_ref[...] = jnp.zeros(acc_ref.shape, F32)
    ones = jnp.ones((ONES_ROWS, ATT_TK), BF16)

    def scores(j, buf, near):
        s_ref, mb_row = buf
        kb = k_ref[pl.ds(pl.multiple_of(j * ATT_TK, ATT_TK), ATT_TK), :]
        base = j * sub_k - qi * sub_q + 2
        for t in range(2):
            kt = kb if diff else kb[:, t * LANES:(t + 1) * LANES]
            s = _dot_nt(kt, qm_ref[t])
            if near:
                s = s + jnp.concatenate(
                    [jnp.concatenate([bias_ref[jnp.clip(base + a - b, 0, 4), t] for b in range(sub_q)], axis=1)
                     for a in range(sub_k)], axis=0)
            s_ref[t] = s
            stat_ref[mb_row + t] = jnp.max(s, axis=0, keepdims=True)

    def update(j, buf, left):
        s_ref, mb_row = buf
        vb = vt_ref[j]
        for t in range(2):
            m_prev = stat_ref[run_max + t]
            c = 0.0 if left is None else jnp.where(left, bias_ref[0, t, 0:1, 0:1], bias_ref[4, t, 0:1, 0:1])
            m_new = jnp.maximum(m_prev, stat_ref[mb_row + t] + c)
            alpha = jnp.exp2(m_prev - m_new)
            p = jnp.exp2(s_ref[t] - (m_new - c)).astype(BF16)
            vv = vb if diff else vb[t * MLA_V:(t + 1) * MLA_V]
            acc_ref[t] = acc_ref[t] * alpha + _dot(jnp.concatenate([vv, ones], axis=0), p)
            stat_ref[run_max + t] = m_new

    bufs = ((sa_ref, max_a), (sb_ref, max_b))
    n_near = ATT_TQ // ATT_TK + 2
    n_far = n_kb - n_near
    w0 = jnp.clip(qi * (ATT_TQ // ATT_TK) - 1, 0, n_far)
    far_block = lambda i: jnp.where(i < w0, i, i + n_near)
    far_buf = lambda parity: bufs[(n_near + parity) % 2]

    scores(w0, bufs[0], True)
    for i in range(1, n_near):
        scores(w0 + i, bufs[i % 2], True)
        update(w0 + i - 1, bufs[(i - 1) % 2], None)
    if n_far == 0:
        update(w0 + n_near - 1, bufs[(n_near - 1) % 2], None)
    else:
        scores(far_block(0), far_buf(0), False)
        update(w0 + n_near - 1, bufs[(n_near - 1) % 2], None)

        def far_step(i, parity):
            j_prev = far_block(i - 1)
            scores(far_block(i), far_buf(parity), False)
            update(j_prev, far_buf(1 - parity), j_prev < w0)

        def two_far_blocks(pair, carry):
            far_step(2 * pair + 1, 1)
            far_step(2 * pair + 2, 0)
            return carry

        lax.fori_loop(0, (n_far - 1) // 2, two_far_blocks, 0)
        if (n_far - 1) % 2:
            far_step(n_far - 1, (n_far - 1) % 2)
        j_last = far_block(n_far - 1)
        update(j_last, far_buf((n_far - 1) % 2), j_last < w0)

    a0 = acc_ref[0]
    a1 = acc_ref[1]
    o0 = a0[:dv] / a0[dv:dv + 1]
    o1 = a1[:dv] / a1[dv:dv + 1]
    if diff:
        lam_dot = lambda r: jnp.sum(vec_ref[r:r + 1, :DIFF_QK] * vec_ref[r + 1:r + 2, :DIFF_QK], axis=-1, keepdims=True)
        lam = jnp.exp(lam_dot(0)) - jnp.exp(lam_dot(2)) + lam_init
        o = (o0 - lam * o1).T
        o = _rms(o, vec_ref[4:5, :]) * (1.0 - lam_init)
    else:
        o = jnp.concatenate([o0, o1], axis=0).T
    o_ref[...] = o.astype(BF16)


def _dense_attn(q, k, vt, bias, vecs, batch, seq_len, diff, lam_init):
    pw = PAIR_W if diff else 2 * LANES
    dv = DIFF_V if diff else MLA_V
    nq = seq_len // ATT_TQ
    nk = seq_len // ATT_TK
    assert ATT_TQ % ATT_TK == 0 and nk >= ATT_TQ // ATT_TK + 2, "the near window must fit in the sequence"
    return pl.pallas_call(
        functools.partial(_dense_attn_kernel, diff=diff, lam_init=lam_init),
        out_shape=jax.ShapeDtypeStruct((batch * seq_len, D_MODEL), BF16),
        grid=(batch, N_BIAS_HEADS // 2, nq),
        in_specs=[
            pl.BlockSpec((ATT_TQ, pw), lambda b, h, i: (b * nq + i, h)),
            pl.BlockSpec((seq_len, pw), lambda b, h, i: (b, h)),
            pl.BlockSpec((nk, PAIR_W, ATT_TK), lambda b, h, i: (b, h, 0)),
            pl.BlockSpec((5, 2, BIAS_TILE, BIAS_TILE), lambda b, h, i: (0, h, 0, 0)),
            pl.BlockSpec(vecs.shape, lambda b, h, i: (0, 0)),
        ],
        out_specs=pl.BlockSpec((ATT_TQ, PAIR_W), lambda b, h, i: (b * nq + i, h)),
        scratch_shapes=[
            pltpu.VMEM((2, ATT_TQ, LANES), BF16),
            pltpu.VMEM((2, ATT_TK, ATT_TQ), F32),
            pltpu.VMEM((2, ATT_TK, ATT_TQ), F32),
            pltpu.VMEM((8, 1, ATT_TQ), F32),
            pltpu.VMEM((2, dv + ONES_ROWS, ATT_TQ), F32),
        ],
        compiler_params=_params("parallel", "parallel", "parallel"),
        name="diff_attn" if diff else "mla_attn",
    )(q, k, vt, bias, vecs)


def _na_attn_kernel(q_ref, k_ref, vt_ref, bias_ref, o_ref, sa_ref, sb_ref, stat_ref):
    n_blocks = q_ref.shape[0] // NA_BLOCK_Q
    chunks = NA_SLAB_Q // NA_BLOCK_Q
    lane = lax.broadcasted_iota(jnp.int32, (NA_BLOCK_Q, PAIR_W), 1)
    first = lane < NA_HEAD_DIM
    ones = jnp.ones((ONES_ROWS, NA_SLAB_Q), BF16)
    slab_chunk = lambda g: jnp.clip(g - 1, 0, n_blocks - chunks)

    def scores(g, buf):
        s_ref, mb_row = buf
        block_type = jnp.where(g == 0, 0, jnp.where(g == n_blocks - 1, 2, 1))
        q = q_ref[pl.ds(pl.multiple_of(g * NA_BLOCK_Q, NA_BLOCK_Q), NA_BLOCK_Q), :]
        ks = k_ref[pl.ds(pl.multiple_of(slab_chunk(g) * NA_BLOCK_Q, NA_BLOCK_Q), NA_SLAB_Q), :]
        zero = jnp.zeros_like(q)
        for t in range(2):
            qt = jnp.where(first if t == 0 else ~first, q, zero)
            s = _dot_nt(ks, qt) + bias_ref[t, block_type]
            s_ref[t] = s
            stat_ref[mb_row + t] = jnp.max(s, axis=0, keepdims=True)

    def finish(g, buf):
        s_ref, mb_row = buf
        c0 = slab_chunk(g)
        vts = jnp.concatenate([vt_ref[c0 + i] for i in range(chunks)], axis=1)
        outs = []
        for t in range(2):
            p = jnp.exp2(s_ref[t] - stat_ref[mb_row + t]).astype(BF16)
            vv = vts[t * NA_HEAD_DIM:(t + 1) * NA_HEAD_DIM]
            acc = _dot(jnp.concatenate([vv, ones], axis=0), p)
            outs.append(acc[:NA_HEAD_DIM] / acc[NA_HEAD_DIM:NA_HEAD_DIM + 1])
        o = jnp.concatenate(outs, axis=0).T
        o_ref[pl.ds(pl.multiple_of(g * NA_BLOCK_Q, NA_BLOCK_Q), NA_BLOCK_Q), :] = o.astype(BF16)

    buf_a = (sa_ref, 0)
    buf_b = (sb_ref, 2)
    scores(0, buf_a)

    def two_blocks(pair, carry):
        g = 2 * pair
        scores(g + 1, buf_b)
        finish(g, buf_a)
        scores(g + 2, buf_a)
        finish(g + 1, buf_b)
        return carry

    lax.fori_loop(0, n_blocks // 2 - 1, two_blocks, 0)
    scores(n_blocks - 1, buf_b)
    finish(n_blocks - 2, buf_a)
    finish(n_blocks - 1, buf_b)


def _na_attn(q, k, vt, bias, batch, seq_len):
    n_blocks = seq_len // NA_BLOCK_Q
    assert n_blocks % 2 == 0 and n_blocks * NA_BLOCK_ROWS >= NA_SLAB_ROWS
    seq = pl.BlockSpec((seq_len, PAIR_W), lambda h, b: (b, h))
    return pl.pallas_call(
        _na_attn_kernel,
        out_shape=jax.ShapeDtypeStruct((batch * seq_len, D_MODEL), BF16),
        grid=(NA_HEADS // 2, batch),
        in_specs=[seq, seq,
                  pl.BlockSpec((n_blocks, PAIR_W, NA_BLOCK_Q), lambda h, b: (b, h, 0)),
                  pl.BlockSpec((2, 3, NA_SLAB_Q, NA_BLOCK_Q), lambda h, b: (h, 0, 0, 0))],
        out_specs=seq,
        scratch_shapes=[
            pltpu.VMEM((2, NA_SLAB_Q, NA_BLOCK_Q), F32),
            pltpu.VMEM((2, NA_SLAB_Q, NA_BLOCK_Q), F32),
            pltpu.VMEM((16, 1, NA_BLOCK_Q), F32),
        ],
        compiler_params=_params("parallel", "parallel"),
        name="na_attn",
    )(q, k, vt, bias)


def _pad_heads(w, heads, dim):
    w = w.reshape(w.shape[0], heads, dim)
    return jnp.pad(w, ((0, 0), (0, 0), (0, LANES - dim))).reshape(w.shape[0], heads * LANES)


def _rot_half_cols(w_rope):
    half = w_rope.shape[-1] // 2
    return jnp.concatenate([-w_rope[..., half:], w_rope[..., :half]], axis=-1)


def _rope_tables(max_len):
    half = MLA_ROPE // 2
    freqs = ROPE_THETA ** (-jnp.arange(half, dtype=F32) / half)
    ang = jnp.arange(max_len, dtype=F32)[:, None] * freqs[None, :]
    cos = jnp.cos(ang)
    sin = jnp.sin(ang)
    pad = jnp.zeros((max_len, LANES - MLA_NOPE - MLA_ROPE), F32)
    cos_t = jnp.concatenate([jnp.ones((max_len, MLA_NOPE), F32), cos, cos, pad], axis=-1)
    sin_t = jnp.concatenate([jnp.zeros((max_len, MLA_NOPE), F32), sin, sin, pad], axis=-1)
    return cos_t, sin_t


def _mla_weights(w_dq, g_q, w_uq, w_dkv, g_kv, w_uk, w_uv, max_len):
    d_qk = MLA_NOPE + MLA_ROPE
    uq = w_uq.reshape(MLA_Q_LORA, MLA_HEADS, d_qk)
    uq_rot = jnp.concatenate(
        [jnp.zeros((MLA_Q_LORA, MLA_HEADS, MLA_NOPE), F32), _rot_half_cols(uq[..., MLA_NOPE:])], axis=-1)
    w_kr = w_dkv[:, MLA_KV_LORA:]
    place = lambda w: jnp.pad(w, ((0, 0), (MLA_NOPE, LANES - d_qk)))
    cos_t, sin_t = _rope_tables(max_len)
    return {
        "wdq": w_dq.astype(BF16),
        "gq": g_q.reshape(1, -1),
        "wuq": _pad_heads(uq.reshape(MLA_Q_LORA, -1), MLA_HEADS, d_qk).astype(BF16),
        "wuq_rot": _pad_heads(uq_rot.reshape(MLA_Q_LORA, -1), MLA_HEADS, d_qk).astype(BF16),
        "wckv": w_dkv[:, :MLA_KV_LORA].astype(BF16),
        "gkv": g_kv.reshape(1, -1),
        "wkr": place(w_kr).astype(BF16),
        "wkr_rot": place(_rot_half_cols(w_kr)).astype(BF16),
        "wuk": _pad_heads(w_uk, MLA_HEADS, MLA_NOPE).astype(BF16),
        "wuvt": w_uv.T.astype(BF16),
        "cos": cos_t,
        "sin": sin_t,
    }


def _ffn_weights(w_gate, w_up, w_down):
    return w_gate.astype(BF16), w_up.astype(BF16), w_down.astype(BF16)


def _diff_vecs(lam_q1, lam_k1, lam_q2, lam_k2, g_sub):
    rows = [jnp.pad(v, (0, LANES - v.shape[0])) for v in (lam_q1, lam_k1, lam_q2, lam_k2)] + [g_sub]
    return jnp.pad(jnp.stack(rows), ((0, VEC_ROWS - len(rows)), (0, 0)))


def _trunk(x, p):
    batch, seq_len, _ = x.shape
    x = x.reshape(batch * seq_len, D_MODEL)
    no_vecs = jnp.zeros((VEC_ROWS, LANES), F32)
    for i in range(DEPTH):
        x = _ffn(x, None, p["norm_g"][i, 0].reshape(1, -1), *p["ffn"][i][0], p["final_g"], False)
        g_mix = p["norm_g"][i, 1].reshape(1, -1)
        m, j = i % N_MIXERS, i // N_MIXERS
        if m == 0:
            w = p["mla"][j]
            q, k, vt = _mla_proj(x, seq_len, g_mix, w)
            a = _dense_attn(q, k, vt, p["t5"], no_vecs, batch, seq_len, False, 0.0)
        elif m == 1:
            w = p["diff"][j]
            lam_init = 0.8 - 0.6 * math.exp(-0.3 * i)
            q, k, vt = _qkv_proj(x, g_mix, w["wq"], w["wk"], w["wvt"], DIFF_QK ** -0.5 * LOG2E, ATT_TK)
            a = _dense_attn(q, k, vt, p["t5"], w["vecs"], batch, seq_len, True, lam_init)
        else:
            w = p["na"][j]
            q, k, vt = _qkv_proj(x, g_mix, w["wq"], w["wk"], w["wvt"], NA_HEAD_DIM ** -0.5 * LOG2E, NA_BLOCK_Q)
            a = _na_attn(q, k, vt, w["bias"], batch, seq_len)
        x = _ffn(x, (a, w["wo"]), p["norm_g"][i, 2].reshape(1, -1), *p["ffn"][i][1], p["final_g"], i == DEPTH - 1)
    return x.reshape(batch, seq_len, D_MODEL)


def kernel(x_prompt, x_sample, norm_g, final_g, ffn_w_gate, ffn_w_up, ffn_w_down, rel_bias_table, mla_w_dq, mla_g_q, mla_w_uq, mla_w_dkv, mla_g_kv, mla_w_uk, mla_w_uv, mla_w_o, diff_w_q, diff_w_k, diff_w_v, diff_lam_q1, diff_lam_k1, diff_lam_q2, diff_lam_k2, diff_g_sub, diff_w_o, na_w_qkv, na_rpb, na_w_o):
    max_len = max(x_prompt.shape[1], x_sample.shape[1])
    hd = NA_HEADS * NA_HEAD_DIM
    p = {
        "norm_g": norm_g,
        "final_g": final_g.reshape(1, -1),
        "ffn": [[_ffn_weights(ffn_w_gate[i, s], ffn_w_up[i, s], ffn_w_down[i, s]) for s in range(2)]
                for i in range(DEPTH)],
        "t5": _t5_tiles(rel_bias_table),
        "mla": [dict(_mla_weights(mla_w_dq[j], mla_g_q[j], mla_w_uq[j], mla_w_dkv[j], mla_g_kv[j],
                                  mla_w_uk[j], mla_w_uv[j], max_len), wo=mla_w_o[j].astype(BF16))
                for j in range(mla_w_dq.shape[0])],
        "diff": [{
            "wq": diff_w_q[j].astype(BF16), "wk": diff_w_k[j].astype(BF16), "wvt": diff_w_v[j].T.astype(BF16),
            "vecs": _diff_vecs(diff_lam_q1[j], diff_lam_k1[j], diff_lam_q2[j], diff_lam_k2[j], diff_g_sub[j]),
            "wo": diff_w_o[j].astype(BF16),
        } for j in range(diff_w_q.shape[0])],
        "na": [{
            "wq": na_w_qkv[j][:, :hd].astype(BF16), "wk": na_w_qkv[j][:, hd:2 * hd].astype(BF16),
            "wvt": na_w_qkv[j][:, 2 * hd:].T.astype(BF16), "bias": _na_bias(na_rpb[j]),
            "wo": na_w_o[j].astype(BF16),
        } for j in range(na_w_qkv.shape[0])],
    }
    return (_trunk(x_prompt, p), _trunk(x_sample, p))
```

```python
import functools
import math

import jax
import jax.numpy as jnp
from jax import lax
from jax.experimental import pallas as pl
from jax.experimental.pallas import tpu as pltpu

F32 = jnp.float32
BF16 = jnp.bfloat16

D_MODEL = 1024
DEPTH = 4
N_MIXERS = 3
GRID_W = 64
RMS_EPS = 1e-6
D_FF = 2816

REL_BUCKETS = 32
N_BIAS_HEADS = 16

MLA_HEADS = 16
MLA_Q_LORA = 512
MLA_KV_LORA = 256
MLA_NOPE = 64
MLA_ROPE = 32
MLA_V = 64
ROPE_THETA = 10000.0

DIFF_HEADS = 8
DIFF_QK = 64
DIFF_V = 2 * DIFF_QK

NA_HEADS = 16
NA_HEAD_DIM = 64
NA_KR = 8
NA_KC = 16

LANES = 128
VMEM_LIMIT_BYTES = 56 * 1024 * 1024

TOK_TILE = 512
FF_CHUNK = 256
ATT_TQ = 1024
ATT_STRIP = 256
FAR_PER_TRIP = 2
VEC_ROWS = 32
ATT_TK = TOK_TILE
BIAS_TILE = LANES
PAIR_W = 2 * NA_HEAD_DIM
NA_BLOCK_ROWS = 4
NA_SLAB_ROWS = NA_BLOCK_ROWS + NA_KR
NA_BLOCK_Q = NA_BLOCK_ROWS * GRID_W
NA_SLAB_Q = NA_SLAB_ROWS * GRID_W
MASK_VALUE = -1e30
ONES_ROWS = 16
LOG2E = math.log2(math.e)

T5_THRESHOLDS = (12, 16, 23, 32, 46, 64, 91)


def _params(*semantics):
    return pltpu.CompilerParams(dimension_semantics=semantics, vmem_limit_bytes=VMEM_LIMIT_BYTES)


def _resident(shape):
    zeros = (0,) * len(shape)
    return pl.BlockSpec(shape, lambda *_: zeros, pipeline_mode=pl.Buffered(1))


def _rms(x, g):
    return x * lax.rsqrt(jnp.mean(x * x, axis=-1, keepdims=True) + RMS_EPS) * g


def _dot(a, b):
    return jnp.dot(a, b, preferred_element_type=F32)


def _dot_nt(a, b):
    return lax.dot_general(a, b, (((1,), (1,)), ((), ())), preferred_element_type=F32)


def _ffn_kernel(*refs, add_mixer, apply_final):
    if add_mixer:
        x_ref, a_ref, wo_ref, g_ref, wg_ref, wu_ref, wd_ref, fg_ref, o_ref = refs
        x = x_ref[...] + _dot(a_ref[...], wo_ref[...])
    else:
        x_ref, g_ref, wg_ref, wu_ref, wd_ref, fg_ref, o_ref = refs
        x = x_ref[...]
    h = _rms(x, g_ref[...]).astype(BF16)
    acc = jnp.zeros(x.shape, F32)
    for c in range(D_FF // FF_CHUNK):
        sl = slice(c * FF_CHUNK, (c + 1) * FF_CHUNK)
        gate = _dot(h, wg_ref[:, sl])
        up = _dot(h, wu_ref[:, sl])
        act = (gate / (1.0 + jnp.exp(-gate))) * up
        acc = acc + _dot(act.astype(BF16), wd_ref[sl, :])
    y = x + 0.5 * acc
    if apply_final:
        y = _rms(y, fg_ref[...])
    o_ref[...] = y


def _ffn(x, mixer, g, wg, wu, wd, final_g, apply_final):
    t = x.shape[0]
    tok = pl.BlockSpec((TOK_TILE, D_MODEL), lambda i: (i, 0))
    mixer_specs = [] if mixer is None else [tok, _resident((D_MODEL, D_MODEL))]
    return pl.pallas_call(
        functools.partial(_ffn_kernel, add_mixer=mixer is not None, apply_final=apply_final),
        out_shape=jax.ShapeDtypeStruct(x.shape, F32),
        grid=(t // TOK_TILE,),
        in_specs=[tok] + mixer_specs + [
            _resident((1, D_MODEL)),
            _resident((D_MODEL, D_FF)),
            _resident((D_MODEL, D_FF)),
            _resident((D_FF, D_MODEL)),
            _resident((1, D_MODEL)),
        ],
        out_specs=tok,
        compiler_params=_params("parallel"),
        name="ffn_mixer" if mixer is not None else "ffn",
    )(x, *(() if mixer is None else mixer), g, wg, wu, wd, final_g)


def _mla_proj_kernel(x_ref, g_ref, wdq_ref, gq_ref, wuq_ref, wuqr_ref, wckv_ref, gkv_ref,
                     wkr_ref, wkrr_ref, wuk_ref, wuvt_ref, cos_ref, sin_ref,
                     q_ref, k_ref, vt_ref):
    hn = _rms(x_ref[...], g_ref[...]).astype(BF16)
    cos = cos_ref[...]
    sin = sin_ref[...]
    scale = (MLA_NOPE + MLA_ROPE) ** -0.5 * LOG2E

    cq = _rms(_dot(hn, wdq_ref[...]), gq_ref[...]).astype(BF16)
    qa = _dot(cq, wuq_ref[...])
    qb = _dot(cq, wuqr_ref[...])
    for h in range(MLA_HEADS):
        sl = slice(h * LANES, (h + 1) * LANES)
        q_ref[:, sl] = ((qa[:, sl] * cos + qb[:, sl] * sin) * scale).astype(BF16)

    ckv = _rms(_dot(hn, wckv_ref[...]), gkv_ref[...]).astype(BF16)
    k_rope = _dot(hn, wkr_ref[...]) * cos + _dot(hn, wkrr_ref[...]) * sin
    k_nope = _dot(ckv, wuk_ref[...])
    for h in range(MLA_HEADS):
        sl = slice(h * LANES, (h + 1) * LANES)
        k_ref[:, sl] = (k_nope[:, sl] + k_rope).astype(BF16)

    vt_ref[0] = _dot_nt(wuvt_ref[...], ckv).astype(BF16)


def _mla_proj(x, seq_len, g, w):
    t = x.shape[0]
    hw = MLA_HEADS * LANES
    pos_blocks = seq_len // TOK_TILE
    tok = lambda width: pl.BlockSpec((TOK_TILE, width), lambda i: (i, 0))
    pos = pl.BlockSpec((TOK_TILE, LANES), lambda i: (i % pos_blocks, 0))
    return pl.pallas_call(
        _mla_proj_kernel,
        out_shape=(
            jax.ShapeDtypeStruct((t, hw), BF16),
            jax.ShapeDtypeStruct((t, hw), BF16),
            jax.ShapeDtypeStruct((t // TOK_TILE, MLA_HEADS * MLA_V, TOK_TILE), BF16),
        ),
        grid=(t // TOK_TILE,),
        in_specs=[
            tok(D_MODEL),
            _resident((1, D_MODEL)),
            _resident((D_MODEL, MLA_Q_LORA)),
            _resident((1, MLA_Q_LORA)),
            _resident((MLA_Q_LORA, hw)),
            _resident((MLA_Q_LORA, hw)),
            _resident((D_MODEL, MLA_KV_LORA)),
            _resident((1, MLA_KV_LORA)),
            _resident((D_MODEL, LANES)),
            _resident((D_MODEL, LANES)),
            _resident((MLA_KV_LORA, hw)),
            _resident((MLA_HEADS * MLA_V, MLA_KV_LORA)),
            pos,
            pos,
        ],
        out_specs=(
            tok(hw),
            tok(hw),
            pl.BlockSpec((1, MLA_HEADS * MLA_V, TOK_TILE), lambda i: (i, 0, 0)),
        ),
        compiler_params=_params("parallel"),
        name="mla_proj",
    )(x, g, w["wdq"], w["gq"], w["wuq"], w["wuq_rot"], w["wckv"], w["gkv"],
      w["wkr"], w["wkr_rot"], w["wuk"], w["wuvt"], w["cos"], w["sin"])


def _qkv_proj_kernel(x_ref, g_ref, wq_ref, wk_ref, wvt_ref, q_ref, k_ref, vt_ref, *, q_scale):
    hn = _rms(x_ref[...], g_ref[...]).astype(BF16)
    q_ref[...] = (_dot(hn, wq_ref[...]) * q_scale).astype(BF16)
    k_ref[...] = _dot(hn, wk_ref[...]).astype(BF16)
    vt = _dot_nt(wvt_ref[...], hn).astype(BF16)
    chunk = vt_ref.shape[-1]
    for c in range(vt_ref.shape[0]):
        vt_ref[c] = vt[:, c * chunk:(c + 1) * chunk]


def _qkv_proj(x, g, wq, wk, wvt, q_scale, v_chunk):
    t = x.shape[0]
    per_tile = TOK_TILE // v_chunk
    tok = pl.BlockSpec((TOK_TILE, D_MODEL), lambda i: (i, 0))
    return pl.pallas_call(
        functools.partial(_qkv_proj_kernel, q_scale=q_scale),
        out_shape=(jax.ShapeDtypeStruct((t, D_MODEL), BF16), jax.ShapeDtypeStruct((t, D_MODEL), BF16),
                   jax.ShapeDtypeStruct((t // v_chunk, D_MODEL, v_chunk), BF16)),
        grid=(t // TOK_TILE,),
        in_specs=[tok, _resident((1, D_MODEL)), _resident((D_MODEL, D_MODEL)),
                  _resident((D_MODEL, D_MODEL)), _resident((D_MODEL, D_MODEL))],
        out_specs=(tok, tok, pl.BlockSpec((per_tile, D_MODEL, v_chunk), lambda i: (i, 0, 0))),
        compiler_params=_params("parallel"),
        name="qkv_proj",
    )(x, g, wq, wk, wvt)


def _t5_tiles_kernel(tab_ref, o_ref):
    head = pl.program_id(0)
    kk = lax.broadcasted_iota(jnp.int32, (BIAS_TILE, BIAS_TILE), 0)
    qq = lax.broadcasted_iota(jnp.int32, (BIAS_TILE, BIAS_TILE), 1)
    half = REL_BUCKETS // 2
    max_exact = half // 2
    for d in range(5):
        rel = (d - 2) * BIAS_TILE + kk - qq
        n = jnp.where(rel < 0, -rel, rel)
        big = jnp.full(rel.shape, max_exact, jnp.int32)
        for thr in T5_THRESHOLDS:
            big = big + jnp.where(n >= thr, 1, 0)
        bucket = jnp.where(rel > 0, half, 0) + jnp.where(n < max_exact, n, big)
        out = jnp.zeros(rel.shape, F32)
        for b in range(REL_BUCKETS):
            out = jnp.where(bucket == b, tab_ref[b, head], out)
        o_ref[d, 0] = out * LOG2E


def _t5_tiles(table):
    return pl.pallas_call(
        _t5_tiles_kernel,
        out_shape=jax.ShapeDtypeStruct((5, N_BIAS_HEADS, BIAS_TILE, BIAS_TILE), F32),
        grid=(N_BIAS_HEADS,),
        in_specs=[pl.BlockSpec(memory_space=pltpu.SMEM)],
        out_specs=pl.BlockSpec((5, 1, BIAS_TILE, BIAS_TILE), lambda h: (0, h, 0, 0)),
        compiler_params=_params("parallel"),
        name="t5_tiles",
    )(table)


def _na_block_delta(block_type, key_row, query_row):
    if block_type == 0:
        return key_row - query_row if key_row < NA_KR else None
    if block_type == 1:
        delta = key_row - query_row - NA_KR // 2
        return delta if -(NA_KR // 2) <= delta < NA_KR // 2 else None
    return key_row - query_row - NA_KR if key_row >= NA_SLAB_ROWS - NA_KR else None


def _na_bias_kernel(rpb_ref, o_ref):
    head = pl.program_id(0)
    n_col = 2 * NA_KC - 1
    shape = (GRID_W, NA_BLOCK_Q)
    kc = lax.broadcasted_iota(jnp.int32, shape, 0)
    col = lax.broadcasted_iota(jnp.int32, shape, 1)
    c = col & (GRID_W - 1)
    block_row = col >> int(math.log2(GRID_W))
    start = jnp.clip(c - NA_KC // 2, 0, GRID_W - NA_KC)
    valid = (kc >= start) & (kc < start + NA_KC)
    col_off = kc - c + (NA_KC - 1)
    masked = jnp.full(shape, MASK_VALUE, F32)
    per_row_offset = []
    for ro in range(2 * NA_KR - 1):
        out = masked
        for co in range(n_col):
            out = jnp.where(valid & (col_off == co), rpb_ref[head, ro * n_col + co] * LOG2E, out)
        per_row_offset.append(out)
    for block_type in range(3):
        for key_row in range(NA_SLAB_ROWS):
            piece = masked
            for query_row in range(NA_BLOCK_ROWS):
                delta = _na_block_delta(block_type, key_row, query_row)
                if delta is not None:
                    piece = jnp.where(block_row == query_row, per_row_offset[delta + NA_KR - 1], piece)
            o_ref[0, block_type, key_row * GRID_W:(key_row + 1) * GRID_W, :] = piece


def _na_bias(rpb):
    n_row = 2 * NA_KR - 1
    n_col = 2 * NA_KC - 1
    return pl.pallas_call(
        _na_bias_kernel,
        out_shape=jax.ShapeDtypeStruct((NA_HEADS, 3, NA_SLAB_Q, NA_BLOCK_Q), F32),
        grid=(NA_HEADS,),
        in_specs=[pl.BlockSpec(memory_space=pltpu.SMEM)],
        out_specs=pl.BlockSpec((1, 3, NA_SLAB_Q, NA_BLOCK_Q), lambda h: (h, 0, 0, 0)),
        compiler_params=_params("parallel"),
        name="na_bias",
    )(rpb.reshape(NA_HEADS, n_row * n_col))


def _dense_attn_kernel(q_ref, k_ref, vt_ref, bias_ref, vec_ref, o_ref, qm_ref, sa_ref, sb_ref, stat_ref, acc_ref,
                       *, diff, lam_init):
    qi = pl.program_id(2)
    n_kb = vt_ref.shape[0]
    sub_k = ATT_TK // BIAS_TILE
    sub_q = ATT_TQ // BIAS_TILE
    dv = DIFF_V if diff else MLA_V
    run_max, max_a, max_b = 0, 2, 4

    q = q_ref[...]
    if diff:
        lane = lax.broadcasted_iota(jnp.int32, q.shape, 1)
        zero = jnp.zeros_like(q)
        qm_ref[0] = jnp.where(lane < DIFF_QK, q, zero)
        qm_ref[1] = jnp.where(lane >= DIFF_QK, q, zero)
    else:
        qm_ref[0] = q[:, :LANES]
        qm_ref[1] = q[:, LANES:]
    stat_ref[...] = jnp.full(stat_ref.shape, MASK_VALUE, F32)
    acc_ref[...] = jnp.zeros(acc_ref.shape, F32)
    ones = jnp.ones((ONES_ROWS, ATT_TK), BF16)

    def scores(j, buf, near):
        s_ref, mb_row = buf
        kb = k_ref[pl.ds(pl.multiple_of(j * ATT_TK, ATT_TK), ATT_TK), :]
        base = j * sub_k - qi * sub_q + 2
        for t in range(2):
            kt = kb if diff else kb[:, t * LANES:(t + 1) * LANES]
            for st in range(ATT_TQ // ATT_STRIP):
                cols = slice(st * ATT_STRIP, (st + 1) * ATT_STRIP)
                s = _dot_nt(kt, qm_ref[t, cols, :])
                if near:
                    s = s + jnp.concatenate(
                        [jnp.concatenate(
                            [bias_ref[jnp.clip(base + a - (st * (ATT_STRIP // BIAS_TILE) + b), 0, 4), t]
                             for b in range(ATT_STRIP // BIAS_TILE)], axis=1)
                         for a in range(sub_k)], axis=0)
                s_ref[t, :, cols] = s
                stat_ref[mb_row + t, :, cols] = jnp.max(s, axis=0, keepdims=True)

    def update(j, buf, left):
        s_ref, mb_row = buf
        vb = vt_ref[j]
        for t in range(2):
            c = 0.0 if left is None else jnp.where(left, bias_ref[0, t, 0:1, 0:1], bias_ref[4, t, 0:1, 0:1])
            vv = vb if diff else vb[t * MLA_V:(t + 1) * MLA_V]
            lhs = jnp.concatenate([vv, ones], axis=0)
            for st in range(ATT_TQ // ATT_STRIP):
                cols = slice(st * ATT_STRIP, (st + 1) * ATT_STRIP)
                m_prev = stat_ref[run_max + t, :, cols]
                m_new = jnp.maximum(m_prev, stat_ref[mb_row + t, :, cols] + c)
                alpha = jnp.exp2(m_prev - m_new)
                p = jnp.exp2(s_ref[t, :, cols] - (m_new - c)).astype(BF16)
                acc_ref[t, :, cols] = acc_ref[t, :, cols] * alpha + _dot(lhs, p)
                stat_ref[run_max + t, :, cols] = m_new

    bufs = ((sa_ref, max_a), (sb_ref, max_b))
    n_near = ATT_TQ // ATT_TK + 2
    n_far = n_kb - n_near
    w0 = jnp.clip(qi * (ATT_TQ // ATT_TK) - 1, 0, n_far)
    far_block = lambda i: jnp.where(i < w0, i, i + n_near)
    far_buf = lambda parity: bufs[(n_near + parity) % 2]

    def near_step(i):
        if i < n_near:
            scores(w0 + i, bufs[i % 2], True)
        elif n_far:
            scores(far_block(0), far_buf(0), False)
        if i > 0:
            update(w0 + i - 1, bufs[(i - 1) % 2], None)

    if n_far:
        for i in range(n_near + 1):
            near_step(i)
    else:
        one_trip = jnp.minimum(qi + 1, 1)
        for i in range(n_near + 1):
            lax.fori_loop(0, one_trip, lambda _, carry, i=i: (near_step(i), carry)[1], 0)
    if n_far:

        def far_step(i, parity):
            j_prev = far_block(i - 1)
            scores(far_block(i), far_buf(parity), False)
            update(j_prev, far_buf(1 - parity), j_prev < w0)

        def far_trip(trip, carry):
            for u in range(1, FAR_PER_TRIP + 1):
                far_step(FAR_PER_TRIP * trip + u, u % 2)
            return carry

        n_trips = (n_far - 1) // FAR_PER_TRIP
        lax.fori_loop(0, n_trips, far_trip, 0)
        for i in range(FAR_PER_TRIP * n_trips + 1, n_far):
            far_step(i, i % 2)
        j_last = far_block(n_far - 1)
        update(j_last, far_buf((n_far - 1) % 2), j_last < w0)

    a0 = acc_ref[0]
    a1 = acc_ref[1]
    o0 = a0[:dv] / a0[dv:dv + 1]
    o1 = a1[:dv] / a1[dv:dv + 1]
    if diff:
        lam_dot = lambda r: jnp.sum(vec_ref[r:r + 1, :DIFF_QK] * vec_ref[r + 1:r + 2, :DIFF_QK], axis=-1, keepdims=True)
        lam = jnp.exp(lam_dot(0)) - jnp.exp(lam_dot(2)) + lam_init
        o = (o0 - lam * o1).T
        o = _rms(o, vec_ref[4:5, :]) * (1.0 - lam_init)
    else:
        o = jnp.concatenate([o0, o1], axis=0).T
    o_ref[...] = o.astype(BF16)


def _dense_attn(q, k, vt, bias, vecs, batch, seq_len, diff, lam_init):
    pw = PAIR_W if diff else 2 * LANES
    dv = DIFF_V if diff else MLA_V
    nq = seq_len // ATT_TQ
    nk = seq_len // ATT_TK
    assert ATT_TQ % ATT_TK == 0 and nk >= ATT_TQ // ATT_TK + 2, "the near window must fit in the sequence"
    return pl.pallas_call(
        functools.partial(_dense_attn_kernel, diff=diff, lam_init=lam_init),
        out_shape=jax.ShapeDtypeStruct((batch * seq_len, D_MODEL), BF16),
        grid=(batch, N_BIAS_HEADS // 2, nq),
        in_specs=[
            pl.BlockSpec((ATT_TQ, pw), lambda b, h, i: (b * nq + i, h)),
            pl.BlockSpec((seq_len, pw), lambda b, h, i: (b, h)),
            pl.BlockSpec((nk, PAIR_W, ATT_TK), lambda b, h, i: (b, h, 0)),
            pl.BlockSpec((5, 2, BIAS_TILE, BIAS_TILE), lambda b, h, i: (0, h, 0, 0)),
            pl.BlockSpec(vecs.shape, lambda b, h, i: (0, 0)),
        ],
        out_specs=pl.BlockSpec((ATT_TQ, PAIR_W), lambda b, h, i: (b * nq + i, h)),
        scratch_shapes=[
            pltpu.VMEM((2, ATT_TQ, LANES), BF16),
            pltpu.VMEM((2, ATT_TK, ATT_TQ), F32),
            pltpu.VMEM((2, ATT_TK, ATT_TQ), F32),
            pltpu.VMEM((8, 1, ATT_TQ), F32),
            pltpu.VMEM((2, dv + ONES_ROWS, ATT_TQ), F32),
        ],
        compiler_params=_params("parallel", "parallel", "parallel"),
        name="diff_attn" if diff else "mla_attn",
    )(q, k, vt, bias, vecs)


def _na_attn_kernel(q_ref, k_ref, vt_ref, bias_ref, o_ref, sa_ref, sb_ref, stat_ref):
    n_blocks = q_ref.shape[0] // NA_BLOCK_Q
    chunks = NA_SLAB_Q // NA_BLOCK_Q
    lane = lax.broadcasted_iota(jnp.int32, (NA_BLOCK_Q, PAIR_W), 1)
    first = lane < NA_HEAD_DIM
    ones = jnp.ones((ONES_ROWS, NA_SLAB_Q), BF16)
    slab_chunk = lambda g: jnp.clip(g - 1, 0, n_blocks - chunks)

    def scores(g, buf):
        s_ref, mb_row = buf
        block_type = jnp.where(g == 0, 0, jnp.where(g == n_blocks - 1, 2, 1))
        q = q_ref[pl.ds(pl.multiple_of(g * NA_BLOCK_Q, NA_BLOCK_Q), NA_BLOCK_Q), :]
        ks = k_ref[pl.ds(pl.multiple_of(slab_chunk(g) * NA_BLOCK_Q, NA_BLOCK_Q), NA_SLAB_Q), :]
        zero = jnp.zeros_like(q)
        for t in range(2):
            qt = jnp.where(first if t == 0 else ~first, q, zero)
            s = _dot_nt(ks, qt) + bias_ref[t, block_type]
            s_ref[t] = s
            stat_ref[mb_row + t] = jnp.max(s, axis=0, keepdims=True)

    def finish(g, buf):
        s_ref, mb_row = buf
        c0 = slab_chunk(g)
        vts = jnp.concatenate([vt_ref[c0 + i] for i in range(chunks)], axis=1)
        outs = []
        for t in range(2):
            p = jnp.exp2(s_ref[t] - stat_ref[mb_row + t]).astype(BF16)
            vv = vts[t * NA_HEAD_DIM:(t + 1) * NA_HEAD_DIM]
            acc = _dot(jnp.concatenate([vv, ones], axis=0), p)
            outs.append(acc[:NA_HEAD_DIM] / acc[NA_HEAD_DIM:NA_HEAD_DIM + 1])
        o = jnp.concatenate(outs, axis=0).T
        o_ref[pl.ds(pl.multiple_of(g * NA_BLOCK_Q, NA_BLOCK_Q), NA_BLOCK_Q), :] = o.astype(BF16)

    buf_a = (sa_ref, 0)
    buf_b = (sb_ref, 2)
    scores(0, buf_a)

    def two_blocks(pair, carry):
        g = 2 * pair
        scores(g + 1, buf_b)
        finish(g, buf_a)
        scores(g + 2, buf_a)
        finish(g + 1, buf_b)
        return carry

    lax.fori_loop(0, n_blocks // 2 - 1, two_blocks, 0)
    scores(n_blocks - 1, buf_b)
    finish(n_blocks - 2, buf_a)
    finish(n_blocks - 1, buf_b)


def _na_attn(q, k, vt, bias, batch, seq_len):
    n_blocks = seq_len // NA_BLOCK_Q
    assert n_blocks % 2 == 0 and n_blocks * NA_BLOCK_ROWS >= NA_SLAB_ROWS
    seq = pl.BlockSpec((seq_len, PAIR_W), lambda h, b: (b, h))
    return pl.pallas_call(
        _na_attn_kernel,
        out_shape=jax.ShapeDtypeStruct((batch * seq_len, D_MODEL), BF16),
        grid=(NA_HEADS // 2, batch),
        in_specs=[seq, seq,
                  pl.BlockSpec((n_blocks, PAIR_W, NA_BLOCK_Q), lambda h, b: (b, h, 0)),
                  pl.BlockSpec((2, 3, NA_SLAB_Q, NA_BLOCK_Q), lambda h, b: (h, 0, 0, 0))],
        out_specs=seq,
        scratch_shapes=[
            pltpu.VMEM((2, NA_SLAB_Q, NA_BLOCK_Q), F32),
            pltpu.VMEM((2, NA_SLAB_Q, NA_BLOCK_Q), F32),
            pltpu.VMEM((16, 1, NA_BLOCK_Q), F32),
        ],
        compiler_params=_params("parallel", "parallel"),
        name="na_attn",
    )(q, k, vt, bias)


def _pad_heads(w, heads, dim):
    w = w.reshape(w.shape[0], heads, dim)
    return jnp.pad(w, ((0, 0), (0, 0), (0, LANES - dim))).reshape(w.shape[0], heads * LANES)


def _rot_half_cols(w_rope):
    half = w_rope.shape[-1] // 2
    return jnp.concatenate([-w_rope[..., half:], w_rope[..., :half]], axis=-1)


def _rope_tables(max_len):
    half = MLA_ROPE // 2
    freqs = ROPE_THETA ** (-jnp.arange(half, dtype=F32) / half)
    ang = jnp.arange(max_len, dtype=F32)[:, None] * freqs[None, :]
    cos = jnp.cos(ang)
    sin = jnp.sin(ang)
    pad = jnp.zeros((max_len, LANES - MLA_NOPE - MLA_ROPE), F32)
    cos_t = jnp.concatenate([jnp.ones((max_len, MLA_NOPE), F32), cos, cos, pad], axis=-1)
    sin_t = jnp.concatenate([jnp.zeros((max_len, MLA_NOPE), F32), sin, sin, pad], axis=-1)
    return cos_t, sin_t


def _mla_weights(w_dq, g_q, w_uq, w_dkv, g_kv, w_uk, w_uv, max_len):
    d_qk = MLA_NOPE + MLA_ROPE
    uq = w_uq.reshape(MLA_Q_LORA, MLA_HEADS, d_qk)
    uq_rot = jnp.concatenate(
        [jnp.zeros((MLA_Q_LORA, MLA_HEADS, MLA_NOPE), F32), _rot_half_cols(uq[..., MLA_NOPE:])], axis=-1)
    w_kr = w_dkv[:, MLA_KV_LORA:]
    place = lambda w: jnp.pad(w, ((0, 0), (MLA_NOPE, LANES - d_qk)))
    cos_t, sin_t = _rope_tables(max_len)
    return {
        "wdq": w_dq.astype(BF16),
        "gq": g_q.reshape(1, -1),
        "wuq": _pad_heads(uq.reshape(MLA_Q_LORA, -1), MLA_HEADS, d_qk).astype(BF16),
        "wuq_rot": _pad_heads(uq_rot.reshape(MLA_Q_LORA, -1), MLA_HEADS, d_qk).astype(BF16),
        "wckv": w_dkv[:, :MLA_KV_LORA].astype(BF16),
        "gkv": g_kv.reshape(1, -1),
        "wkr": place(w_kr).astype(BF16),
        "wkr_rot": place(_rot_half_cols(w_kr)).astype(BF16),
        "wuk": _pad_heads(w_uk, MLA_HEADS, MLA_NOPE).astype(BF16),
        "wuvt": w_uv.T.astype(BF16),
        "cos": cos_t,
        "sin": sin_t,
    }


def _ffn_weights(w_gate, w_up, w_down):
    return w_gate.astype(BF16), w_up.astype(BF16), w_down.astype(BF16)


def _diff_vecs(lam_q1, lam_k1, lam_q2, lam_k2, g_sub):
    rows = [jnp.pad(v, (0, LANES - v.shape[0])) for v in (lam_q1, lam_k1, lam_q2, lam_k2)] + [g_sub]
    return jnp.pad(jnp.stack(rows), ((0, VEC_ROWS - len(rows)), (0, 0)))


def _trunk(x, p):
    batch, seq_len, _ = x.shape
    x = x.reshape(batch * seq_len, D_MODEL)
    no_vecs = jnp.zeros((VEC_ROWS, LANES), F32)
    for i in range(DEPTH):
        x = _ffn(x, None, p["norm_g"][i, 0].reshape(1, -1), *p["ffn"][i][0], p["final_g"], False)
        g_mix = p["norm_g"][i, 1].reshape(1, -1)
        m, j = i % N_MIXERS, i // N_MIXERS
        if m == 0:
            w = p["mla"][j]
            q, k, vt = _mla_proj(x, seq_len, g_mix, w)
            a = _dense_attn(q, k, vt, p["t5"], no_vecs, batch, seq_len, False, 0.0)
        elif m == 1:
            w = p["diff"][j]
            lam_init = 0.8 - 0.6 * math.exp(-0.3 * i)
            q, k, vt = _qkv_proj(x, g_mix, w["wq"], w["wk"], w["wvt"], DIFF_QK ** -0.5 * LOG2E, ATT_TK)
            a = _dense_attn(q, k, vt, p["t5"], w["vecs"], batch, seq_len, True, lam_init)
        else:
            w = p["na"][j]
            q, k, vt = _qkv_proj(x, g_mix, w["wq"], w["wk"], w["wvt"], NA_HEAD_DIM ** -0.5 * LOG2E, NA_BLOCK_Q)
            a = _na_attn(q, k, vt, w["bias"], batch, seq_len)
        x = _ffn(x, (a, w["wo"]), p["norm_g"][i, 2].reshape(1, -1), *p["ffn"][i][1], p["final_g"], i == DEPTH - 1)
    return x.reshape(batch, seq_len, D_MODEL)


def kernel(x_prompt, x_sample, norm_g, final_g, ffn_w_gate, ffn_w_up, ffn_w_down, rel_bias_table, mla_w_dq, mla_g_q, mla_w_uq, mla_w_dkv, mla_g_kv, mla_w_uk, mla_w_uv, mla_w_o, diff_w_q, diff_w_k, diff_w_v, diff_lam_q1, diff_lam_k1, diff_lam_q2, diff_lam_k2, diff_g_sub, diff_w_o, na_w_qkv, na_rpb, na_w_o):
    max_len = max(x_prompt.shape[1], x_sample.shape[1])
    hd = NA_HEADS * NA_HEAD_DIM
    p = {
        "norm_g": norm_g,
        "final_g": final_g.reshape(1, -1),
        "ffn": [[_ffn_weights(ffn_w_gate[i, s], ffn_w_up[i, s], ffn_w_down[i, s]) for s in range(2)]
                for i in range(DEPTH)],
        "t5": _t5_tiles(rel_bias_table),
        "mla": [dict(_mla_weights(mla_w_dq[j], mla_g_q[j], mla_w_uq[j], mla_w_dkv[j], mla_g_kv[j],
                                  mla_w_uk[j], mla_w_uv[j], max_len), wo=mla_w_o[j].astype(BF16))
                for j in range(mla_w_dq.shape[0])],
        "diff": [{
            "wq": diff_w_q[j].astype(BF16), "wk": diff_w_k[j].astype(BF16), "wvt": diff_w_v[j].T.astype(BF16),
            "vecs": _diff_vecs(diff_lam_q1[j], diff_lam_k1[j], diff_lam_q2[j], diff_lam_k2[j], diff_g_sub[j]),
            "wo": diff_w_o[j].astype(BF16),
        } for j in range(diff_w_q.shape[0])],
        "na": [{
            "wq": na_w_qkv[j][:, :hd].astype(BF16), "wk": na_w_qkv[j][:, hd:2 * hd].astype(BF16),
            "wvt": na_w_qkv[j][:, 2 * hd:].T.astype(BF16), "bias": _na_bias(na_rpb[j]),
            "wo": na_w_o[j].astype(BF16),
        } for j in range(na_w_qkv.shape[0])],
    }
    return (_trunk(x_prompt, p), _trunk(x_sample, p))
```

```python
import functools
import math

import jax
import jax.numpy as jnp
from jax import lax
from jax.experimental import pallas as pl
from jax.experimental.pallas import tpu as pltpu

F32 = jnp.float32
BF16 = jnp.bfloat16

D_MODEL = 1024
DEPTH = 4
N_MIXERS = 3
GRID_W = 64
RMS_EPS = 1e-6
D_FF = 2816

REL_BUCKETS = 32
N_BIAS_HEADS = 16

MLA_HEADS = 16
MLA_Q_LORA = 512
MLA_KV_LORA = 256
MLA_NOPE = 64
MLA_ROPE = 32
MLA_V = 64
ROPE_THETA = 10000.0

DIFF_HEADS = 8
DIFF_QK = 64
DIFF_V = 2 * DIFF_QK

NA_HEADS = 16
NA_HEAD_DIM = 64
NA_KR = 8
NA_KC = 16

LANES = 128
VMEM_LIMIT_BYTES = 56 * 1024 * 1024

TOK_TILE = 512
FF_CHUNK = 256
ATT_TQ = 1024
ATT_STRIP = 256
FAR_PER_TRIP = 2
VEC_ROWS = 32
ATT_TK = TOK_TILE
BIAS_TILE = LANES
PAIR_W = 2 * NA_HEAD_DIM
NA_BLOCK_ROWS = 4
NA_SLAB_ROWS = NA_BLOCK_ROWS + NA_KR
NA_BLOCK_Q = NA_BLOCK_ROWS * GRID_W
NA_SLAB_Q = NA_SLAB_ROWS * GRID_W
MASK_VALUE = -1e30
ONES_ROWS = 16
LOG2E = math.log2(math.e)

T5_THRESHOLDS = (12, 16, 23, 32, 46, 64, 91)


def _params(*semantics):
    return pltpu.CompilerParams(dimension_semantics=semantics, vmem_limit_bytes=VMEM_LIMIT_BYTES)


def _resident(shape):
    zeros = (0,) * len(shape)
    return pl.BlockSpec(shape, lambda *_: zeros, pipeline_mode=pl.Buffered(1))


def _rms(x, g):
    return x * lax.rsqrt(jnp.mean(x * x, axis=-1, keepdims=True) + RMS_EPS) * g


def _dot(a, b):
    return jnp.dot(a, b, preferred_element_type=F32)


def _dot_nt(a, b):
    return lax.dot_general(a, b, (((1,), (1,)), ((), ())), preferred_element_type=F32)


def _ffn_kernel(*refs, add_mixer, apply_final):
    if add_mixer:
        x_ref, a_ref, wo_ref, g_ref, wg_ref, wu_ref, wd_ref, fg_ref, o_ref = refs
        x = x_ref[...] + _dot(a_ref[...], wo_ref[...])
    else:
        x_ref, g_ref, wg_ref, wu_ref, wd_ref, fg_ref, o_ref = refs
        x = x_ref[...]
    h = _rms(x, g_ref[...]).astype(BF16)
    acc = jnp.zeros(x.shape, F32)
    for c in range(D_FF // FF_CHUNK):
        sl = slice(c * FF_CHUNK, (c + 1) * FF_CHUNK)
        gate = _dot(h, wg_ref[:, sl])
        up = _dot(h, wu_ref[:, sl])
        act = (gate / (1.0 + jnp.exp(-gate))) * up
        acc = acc + _dot(act.astype(BF16), wd_ref[sl, :])
    y = x + 0.5 * acc
    if apply_final:
        y = _rms(y, fg_ref[...])
    o_ref[...] = y


def _ffn(x, mixer, g, wg, wu, wd, final_g, apply_final):
    t = x.shape[0]
    tok = pl.BlockSpec((TOK_TILE, D_MODEL), lambda i: (i, 0))
    mixer_specs = [] if mixer is None else [tok, _resident((D_MODEL, D_MODEL))]
    return pl.pallas_call(
        functools.partial(_ffn_kernel, add_mixer=mixer is not None, apply_final=apply_final),
        out_shape=jax.ShapeDtypeStruct(x.shape, F32),
        grid=(t // TOK_TILE,),
        in_specs=[tok] + mixer_specs + [
            _resident((1, D_MODEL)),
            _resident((D_MODEL, D_FF)),
            _resident((D_MODEL, D_FF)),
            _resident((D_FF, D_MODEL)),
            _resident((1, D_MODEL)),
        ],
        out_specs=tok,
        compiler_params=_params("parallel"),
        name="ffn_mixer" if mixer is not None else "ffn",
    )(x, *(() if mixer is None else mixer), g, wg, wu, wd, final_g)


def _mla_proj_kernel(x_ref, g_ref, wdq_ref, gq_ref, wuq_ref, wuqr_ref, wckv_ref, gkv_ref,
                     wkr_ref, wkrr_ref, wuk_ref, wuvt_ref, cos_ref, sin_ref,
                     q_ref, k_ref, vt_ref):
    hn = _rms(x_ref[...], g_ref[...]).astype(BF16)
    cos = cos_ref[...]
    sin = sin_ref[...]
    scale = (MLA_NOPE + MLA_ROPE) ** -0.5 * LOG2E

    cq = _rms(_dot(hn, wdq_ref[...]), gq_ref[...]).astype(BF16)
    qa = _dot(cq, wuq_ref[...])
    qb = _dot(cq, wuqr_ref[...])
    for h in range(MLA_HEADS):
        sl = slice(h * LANES, (h + 1) * LANES)
        q_ref[:, sl] = ((qa[:, sl] * cos + qb[:, sl] * sin) * scale).astype(BF16)

    ckv = _rms(_dot(hn, wckv_ref[...]), gkv_ref[...]).astype(BF16)
    k_rope = _dot(hn, wkr_ref[...]) * cos + _dot(hn, wkrr_ref[...]) * sin
    k_nope = _dot(ckv, wuk_ref[...])
    for h in range(MLA_HEADS):
        sl = slice(h * LANES, (h + 1) * LANES)
        k_ref[:, sl] = (k_nope[:, sl] + k_rope).astype(BF16)

    vt_ref[0] = _dot_nt(wuvt_ref[...], ckv).astype(BF16)


def _mla_proj(x, seq_len, g, w):
    t = x.shape[0]
    hw = MLA_HEADS * LANES
    pos_blocks = seq_len // TOK_TILE
    tok = lambda width: pl.BlockSpec((TOK_TILE, width), lambda i: (i, 0))
    pos = pl.BlockSpec((TOK_TILE, LANES), lambda i: (i % pos_blocks, 0))
    return pl.pallas_call(
        _mla_proj_kernel,
        out_shape=(
            jax.ShapeDtypeStruct((t, hw), BF16),
            jax.ShapeDtypeStruct((t, hw), BF16),
            jax.ShapeDtypeStruct((t // TOK_TILE, MLA_HEADS * MLA_V, TOK_TILE), BF16),
        ),
        grid=(t // TOK_TILE,),
        in_specs=[
            tok(D_MODEL),
            _resident((1, D_MODEL)),
            _resident((D_MODEL, MLA_Q_LORA)),
            _resident((1, MLA_Q_LORA)),
            _resident((MLA_Q_LORA, hw)),
            _resident((MLA_Q_LORA, hw)),
            _resident((D_MODEL, MLA_KV_LORA)),
            _resident((1, MLA_KV_LORA)),
            _resident((D_MODEL, LANES)),
            _resident((D_MODEL, LANES)),
            _resident((MLA_KV_LORA, hw)),
            _resident((MLA_HEADS * MLA_V, MLA_KV_LORA)),
            pos,
            pos,
        ],
        out_specs=(
            tok(hw),
            tok(hw),
            pl.BlockSpec((1, MLA_HEADS * MLA_V, TOK_TILE), lambda i: (i, 0, 0)),
        ),
        compiler_params=_params("parallel"),
        name="mla_proj",
    )(x, g, w["wdq"], w["gq"], w["wuq"], w["wuq_rot"], w["wckv"], w["gkv"],
      w["wkr"], w["wkr_rot"], w["wuk"], w["wuvt"], w["cos"], w["sin"])


def _qkv_proj_kernel(x_ref, g_ref, wq_ref, wk_ref, wvt_ref, q_ref, k_ref, vt_ref, *, q_scale):
    hn = _rms(x_ref[...], g_ref[...]).astype(BF16)
    q_ref[...] = (_dot(hn, wq_ref[...]) * q_scale).astype(BF16)
    k_ref[...] = _dot(hn, wk_ref[...]).astype(BF16)
    vt = _dot_nt(wvt_ref[...], hn).astype(BF16)
    chunk = vt_ref.shape[-1]
    for c in range(vt_ref.shape[0]):
        vt_ref[c] = vt[:, c * chunk:(c + 1) * chunk]


def _qkv_proj(x, g, wq, wk, wvt, q_scale, v_chunk):
    t = x.shape[0]
    per_tile = TOK_TILE // v_chunk
    tok = pl.BlockSpec((TOK_TILE, D_MODEL), lambda i: (i, 0))
    return pl.pallas_call(
        functools.partial(_qkv_proj_kernel, q_scale=q_scale),
        out_shape=(jax.ShapeDtypeStruct((t, D_MODEL), BF16), jax.ShapeDtypeStruct((t, D_MODEL), BF16),
                   jax.ShapeDtypeStruct((t // v_chunk, D_MODEL, v_chunk), BF16)),
        grid=(t // TOK_TILE,),
        in_specs=[tok, _resident((1, D_MODEL)), _resident((D_MODEL, D_MODEL)),
                  _resident((D_MODEL, D_MODEL)), _resident((D_MODEL, D_MODEL))],
        out_specs=(tok, tok, pl.BlockSpec((per_tile, D_MODEL, v_chunk), lambda i: (i, 0, 0))),
        compiler_params=_params("parallel"),
        name="qkv_proj",
    )(x, g, wq, wk, wvt)


def _t5_tiles_kernel(tab_ref, o_ref):
    head = pl.program_id(0)
    kk = lax.broadcasted_iota(jnp.int32, (BIAS_TILE, BIAS_TILE), 0)
    qq = lax.broadcasted_iota(jnp.int32, (BIAS_TILE, BIAS_TILE), 1)
    half = REL_BUCKETS // 2
    max_exact = half // 2
    for d in range(5):
        rel = (d - 2) * BIAS_TILE + kk - qq
        n = jnp.where(rel < 0, -rel, rel)
        big = jnp.full(rel.shape, max_exact, jnp.int32)
        for thr in T5_THRESHOLDS:
            big = big + jnp.where(n >= thr, 1, 0)
        bucket = jnp.where(rel > 0, half, 0) + jnp.where(n < max_exact, n, big)
        out = jnp.zeros(rel.shape, F32)
        for b in range(REL_BUCKETS):
            out = jnp.where(bucket == b, tab_ref[b, head], out)
        o_ref[d, 0] = out * LOG2E


def _t5_tiles(table):
    return pl.pallas_call(
        _t5_tiles_kernel,
        out_shape=jax.ShapeDtypeStruct((5, N_BIAS_HEADS, BIAS_TILE, BIAS_TILE), F32),
        grid=(N_BIAS_HEADS,),
        in_specs=[pl.BlockSpec(memory_space=pltpu.SMEM)],
        out_specs=pl.BlockSpec((5, 1, BIAS_TILE, BIAS_TILE), lambda h: (0, h, 0, 0)),
        compiler_params=_params("parallel"),
        name="t5_tiles",
    )(table)


def _na_block_delta(block_type, key_row, query_row):
    if block_type == 0:
        return key_row - query_row if key_row < NA_KR else None
    if block_type == 1:
        delta = key_row - query_row - NA_KR // 2
        return delta if -(NA_KR // 2) <= delta < NA_KR // 2 else None
    return key_row - query_row - NA_KR if key_row >= NA_SLAB_ROWS - NA_KR else None


def _na_bias_kernel(rpb_ref, o_ref):
    head = pl.program_id(0)
    n_col = 2 * NA_KC - 1
    shape = (GRID_W, NA_BLOCK_Q)
    kc = lax.broadcasted_iota(jnp.int32, shape, 0)
    col = lax.broadcasted_iota(jnp.int32, shape, 1)
    c = col & (GRID_W - 1)
    block_row = col >> int(math.log2(GRID_W))
    start = jnp.clip(c - NA_KC // 2, 0, GRID_W - NA_KC)
    valid = (kc >= start) & (kc < start + NA_KC)
    col_off = kc - c + (NA_KC - 1)
    masked = jnp.full(shape, MASK_VALUE, F32)
    per_row_offset = []
    for ro in range(2 * NA_KR - 1):
        out = masked
        for co in range(n_col):
            out = jnp.where(valid & (col_off == co), rpb_ref[head, ro * n_col + co] * LOG2E, out)
        per_row_offset.append(out)
    for block_type in range(3):
        for key_row in range(NA_SLAB_ROWS):
            piece = masked
            for query_row in range(NA_BLOCK_ROWS):
                delta = _na_block_delta(block_type, key_row, query_row)
                if delta is not None:
                    piece = jnp.where(block_row == query_row, per_row_offset[delta + NA_KR - 1], piece)
            o_ref[0, block_type, key_row * GRID_W:(key_row + 1) * GRID_W, :] = piece


def _na_bias(rpb):
    n_row = 2 * NA_KR - 1
    n_col = 2 * NA_KC - 1
    return pl.pallas_call(
        _na_bias_kernel,
        out_shape=jax.ShapeDtypeStruct((NA_HEADS, 3, NA_SLAB_Q, NA_BLOCK_Q), F32),
        grid=(NA_HEADS,),
        in_specs=[pl.BlockSpec(memory_space=pltpu.SMEM)],
        out_specs=pl.BlockSpec((1, 3, NA_SLAB_Q, NA_BLOCK_Q), lambda h: (h, 0, 0, 0)),
        compiler_params=_params("parallel"),
        name="na_bias",
    )(rpb.reshape(NA_HEADS, n_row * n_col))


def _dense_attn_kernel(q_ref, k_ref, vt_ref, bias_ref, vec_ref, o_ref, qm_ref, sa_ref, sb_ref, stat_ref, acc_ref,
                       *, diff, lam_init, n_qb):
    qi = pl.program_id(2)
    n_kb = vt_ref.shape[0]
    sub_k = ATT_TK // BIAS_TILE
    dv = DIFF_V if diff else MLA_V
    run_max, max_a, max_b = 0, 2, 4

    q = q_ref[...]
    if diff:
        lane = lax.broadcasted_iota(jnp.int32, q.shape, 1)
        zero = jnp.zeros_like(q)
        qm_ref[0] = jnp.where(lane < DIFF_QK, q, zero)
        qm_ref[1] = jnp.where(lane >= DIFF_QK, q, zero)
    else:
        qm_ref[0] = q[:, :LANES]
        qm_ref[1] = q[:, LANES:]
    stat_ref[...] = jnp.full(stat_ref.shape, MASK_VALUE, F32)
    acc_ref[...] = jnp.zeros(acc_ref.shape, F32)
    ones = jnp.ones((ONES_ROWS, ATT_TK), BF16)

    tile_rows = lambda a: slice(a * BIAS_TILE, (a + 1) * BIAS_TILE)
    side_const = lambda t: (bias_ref[0, t, 0:1, 0:1], bias_ref[4, t, 0:1, 0:1])

    def scores(j, buf, off):
        s_ref, mb_row = buf
        kb = k_ref[pl.ds(pl.multiple_of(j * ATT_TK, ATT_TK), ATT_TK), :]
        for t in range(2):
            kt = kb if diff else kb[:, t * LANES:(t + 1) * LANES]
            c_left, c_right = side_const(t)
            for st in range(ATT_TQ // ATT_STRIP):
                cols = slice(st * ATT_STRIP, (st + 1) * ATT_STRIP)
                s = _dot_nt(kt, qm_ref[t, cols, :])
                if off is None:
                    s_ref[t, :, cols] = s
                    stat_ref[mb_row + t, :, cols] = jnp.max(s, axis=0, keepdims=True)
                    continue
                for bb in range(ATT_STRIP // BIAS_TILE):
                    b = st * (ATT_STRIP // BIAS_TILE) + bb
                    best = None
                    for a in range(sub_k):
                        piece = s[tile_rows(a), tile_rows(bb)]
                        o = off(a, b)
                        if abs(o) <= 1:
                            piece = piece + bias_ref[o + 2, t]
                            piece_max = jnp.max(piece, axis=0, keepdims=True)
                        else:
                            piece_max = jnp.max(piece, axis=0, keepdims=True) + (c_left if o < 0 else c_right)
                        s_ref[t, tile_rows(a), tile_rows(b)] = piece
                        best = piece_max if best is None else jnp.maximum(best, piece_max)
                    stat_ref[mb_row + t, :, tile_rows(b)] = best

    def update(j, buf, left, off):
        s_ref, mb_row = buf
        vb = vt_ref[j]
        for t in range(2):
            c_left, c_right = side_const(t)
            vv = vb if diff else vb[t * MLA_V:(t + 1) * MLA_V]
            lhs = jnp.concatenate([vv, ones], axis=0)
            for st in range(ATT_TQ // ATT_STRIP):
                cols = slice(st * ATT_STRIP, (st + 1) * ATT_STRIP)
                m_prev = stat_ref[run_max + t, :, cols]
                if off is None:
                    c = jnp.where(left, c_left, c_right)
                    m_new = jnp.maximum(m_prev, stat_ref[mb_row + t, :, cols] + c)
                    p = jnp.exp2(s_ref[t, :, cols] - (m_new - c)).astype(BF16)
                else:
                    m_new = jnp.maximum(m_prev, stat_ref[mb_row + t, :, cols])
                    p_cols = []
                    for bb in range(ATT_STRIP // BIAS_TILE):
                        b = st * (ATT_STRIP // BIAS_TILE) + bb
                        m_b = m_new[:, tile_rows(bb)]
                        shift = {-1: m_b - c_left, 0: m_b, 1: m_b - c_right}
                        p_cols.append(jnp.concatenate(
                            [jnp.exp2(s_ref[t, tile_rows(a), tile_rows(b)]
                                      - shift[0 if abs(off(a, b)) <= 1 else (1 if off(a, b) > 0 else -1)])
                             for a in range(sub_k)], axis=0))
                    p = jnp.concatenate(p_cols, axis=1).astype(BF16)
                alpha = jnp.exp2(m_prev - m_new)
                acc_ref[t, :, cols] = acc_ref[t, :, cols] * alpha + _dot(lhs, p)
                stat_ref[run_max + t, :, cols] = m_new

    bufs = ((sa_ref, max_a), (sb_ref, max_b))
    n_near = ATT_TQ // ATT_TK + 2
    n_far = n_kb - n_near
    w0 = jnp.clip(qi * (ATT_TQ // ATT_TK) - 1, 0, n_far)
    far_block = lambda i: jnp.where(i < w0, i, i + n_near)
    far_buf = lambda parity: bufs[(n_near + parity) % 2]

    def near_phase(delta):
        def tile_offset(i):
            return lambda a, b: (i + delta) * sub_k + a - b

        def near_step(i):
            if i < n_near:
                scores(w0 + i, bufs[i % 2], tile_offset(i))
            elif n_far:
                scores(far_block(0), far_buf(0), None)
            if i > 0:
                update(w0 + i - 1, bufs[(i - 1) % 2], None, tile_offset(i - 1))

        one_trip = jnp.minimum(qi + 1, 1)

        def region(first, last):
            def body(_, carry):
                for i in range(first, last):
                    near_step(i)
                return carry
            lax.fori_loop(0, one_trip, body, 0)

        region(0, 3)
        for first in range(3, n_near + 1, 2):
            region(first, min(first + 2, n_near + 1))

    pl.when(qi == 0)(lambda: near_phase(0))
    pl.when(qi == n_qb - 1)(lambda: near_phase(-2))
    if n_qb > 2:
        pl.when((qi > 0) & (qi < n_qb - 1))(lambda: near_phase(-1))
    if n_far:

        def far_step(i, parity):
            j_prev = far_block(i - 1)
            scores(far_block(i), far_buf(parity), None)
            update(j_prev, far_buf(1 - parity), j_prev < w0, None)

        def far_trip(trip, carry):
            for u in range(1, FAR_PER_TRIP + 1):
                far_step(FAR_PER_TRIP * trip + u, u % 2)
            return carry

        n_trips = (n_far - 1) // FAR_PER_TRIP
        lax.fori_loop(0, n_trips, far_trip, 0)
        for i in range(FAR_PER_TRIP * n_trips + 1, n_far):
            far_step(i, i % 2)
        j_last = far_block(n_far - 1)
        update(j_last, far_buf((n_far - 1) % 2), j_last < w0, None)

    a0 = acc_ref[0]
    a1 = acc_ref[1]
    o0 = a0[:dv] / a0[dv:dv + 1]
    o1 = a1[:dv] / a1[dv:dv + 1]
    if diff:
        lam_dot = lambda r: jnp.sum(vec_ref[r:r + 1, :DIFF_QK] * vec_ref[r + 1:r + 2, :DIFF_QK], axis=-1, keepdims=True)
        lam = jnp.exp(lam_dot(0)) - jnp.exp(lam_dot(2)) + lam_init
        o = (o0 - lam * o1).T
        o = _rms(o, vec_ref[4:5, :]) * (1.0 - lam_init)
    else:
        o = jnp.concatenate([o0, o1], axis=0).T
    o_ref[...] = o.astype(BF16)


def _dense_attn(q, k, vt, bias, vecs, batch, seq_len, diff, lam_init):
    pw = PAIR_W if diff else 2 * LANES
    dv = DIFF_V if diff else MLA_V
    nq = seq_len // ATT_TQ
    nk = seq_len // ATT_TK
    assert ATT_TQ % ATT_TK == 0 and nk >= ATT_TQ // ATT_TK + 2, "the near window must fit in the sequence"
    assert nq >= 2, "first and last query blocks use different near windows"
    return pl.pallas_call(
        functools.partial(_dense_attn_kernel, diff=diff, lam_init=lam_init, n_qb=nq),
        out_shape=jax.ShapeDtypeStruct((batch * seq_len, D_MODEL), BF16),
        grid=(batch, N_BIAS_HEADS // 2, nq),
        in_specs=[
            pl.BlockSpec((ATT_TQ, pw), lambda b, h, i: (b * nq + i, h)),
            pl.BlockSpec((seq_len, pw), lambda b, h, i: (b, h)),
            pl.BlockSpec((nk, PAIR_W, ATT_TK), lambda b, h, i: (b, h, 0)),
            pl.BlockSpec((5, 2, BIAS_TILE, BIAS_TILE), lambda b, h, i: (0, h, 0, 0)),
            pl.BlockSpec(vecs.shape, lambda b, h, i: (0, 0)),
        ],
        out_specs=pl.BlockSpec((ATT_TQ, PAIR_W), lambda b, h, i: (b * nq + i, h)),
        scratch_shapes=[
            pltpu.VMEM((2, ATT_TQ, LANES), BF16),
            pltpu.VMEM((2, ATT_TK, ATT_TQ), F32),
            pltpu.VMEM((2, ATT_TK, ATT_TQ), F32),
            pltpu.VMEM((8, 1, ATT_TQ), F32),
            pltpu.VMEM((2, dv + ONES_ROWS, ATT_TQ), F32),
        ],
        compiler_params=_params("parallel", "parallel", "parallel"),
        name="diff_attn" if diff else "mla_attn",
    )(q, k, vt, bias, vecs)


def _na_attn_kernel(q_ref, k_ref, vt_ref, bias_ref, o_ref, sa_ref, sb_ref, stat_ref):
    n_blocks = q_ref.shape[0] // NA_BLOCK_Q
    chunks = NA_SLAB_Q // NA_BLOCK_Q
    lane = lax.broadcasted_iota(jnp.int32, (NA_BLOCK_Q, PAIR_W), 1)
    first = lane < NA_HEAD_DIM
    ones = jnp.ones((ONES_ROWS, NA_SLAB_Q), BF16)
    slab_chunk = lambda g: jnp.clip(g - 1, 0, n_blocks - chunks)

    def scores(g, buf):
        s_ref, mb_row = buf
        block_type = jnp.where(g == 0, 0, jnp.where(g == n_blocks - 1, 2, 1))
        q = q_ref[pl.ds(pl.multiple_of(g * NA_BLOCK_Q, NA_BLOCK_Q), NA_BLOCK_Q), :]
        ks = k_ref[pl.ds(pl.multiple_of(slab_chunk(g) * NA_BLOCK_Q, NA_BLOCK_Q), NA_SLAB_Q), :]
        zero = jnp.zeros_like(q)
        for t in range(2):
            qt = jnp.where(first if t == 0 else ~first, q, zero)
            s = _dot_nt(ks, qt) + bias_ref[t, block_type]
            s_ref[t] = s
            stat_ref[mb_row + t] = jnp.max(s, axis=0, keepdims=True)

    def finish(g, buf):
        s_ref, mb_row = buf
        c0 = slab_chunk(g)
        vts = jnp.concatenate([vt_ref[c0 + i] for i in range(chunks)], axis=1)
        outs = []
        for t in range(2):
            p = jnp.exp2(s_ref[t] - stat_ref[mb_row + t]).astype(BF16)
            vv = vts[t * NA_HEAD_DIM:(t + 1) * NA_HEAD_DIM]
            acc = _dot(jnp.concatenate([vv, ones], axis=0), p)
            outs.append(acc[:NA_HEAD_DIM] / acc[NA_HEAD_DIM:NA_HEAD_DIM + 1])
        o = jnp.concatenate(outs, axis=0).T
        o_ref[pl.ds(pl.multiple_of(g * NA_BLOCK_Q, NA_BLOCK_Q), NA_BLOCK_Q), :] = o.astype(BF16)

    buf_a = (sa_ref, 0)
    buf_b = (sb_ref, 2)
    scores(0, buf_a)

    def two_blocks(pair, carry):
        g = 2 * pair
        scores(g + 1, buf_b)
        finish(g, buf_a)
        scores(g + 2, buf_a)
        finish(g + 1, buf_b)
        return carry

    lax.fori_loop(0, n_blocks // 2 - 1, two_blocks, 0)
    scores(n_blocks - 1, buf_b)
    finish(n_blocks - 2, buf_a)
    finish(n_blocks - 1, buf_b)


def _na_attn(q, k, vt, bias, batch, seq_len):
    n_blocks = seq_len // NA_BLOCK_Q
    assert n_blocks % 2 == 0 and n_blocks * NA_BLOCK_ROWS >= NA_SLAB_ROWS
    seq = pl.BlockSpec((seq_len, PAIR_W), lambda h, b: (b, h))
    return pl.pallas_call(
        _na_attn_kernel,
        out_shape=jax.ShapeDtypeStruct((batch * seq_len, D_MODEL), BF16),
        grid=(NA_HEADS // 2, batch),
        in_specs=[seq, seq,
                  pl.BlockSpec((n_blocks, PAIR_W, NA_BLOCK_Q), lambda h, b: (b, h, 0)),
                  pl.BlockSpec((2, 3, NA_SLAB_Q, NA_BLOCK_Q), lambda h, b: (h, 0, 0, 0))],
        out_specs=seq,
        scratch_shapes=[
            pltpu.VMEM((2, NA_SLAB_Q, NA_BLOCK_Q), F32),
            pltpu.VMEM((2, NA_SLAB_Q, NA_BLOCK_Q), F32),
            pltpu.VMEM((16, 1, NA_BLOCK_Q), F32),
        ],
        compiler_params=_params("parallel", "parallel"),
        name="na_attn",
    )(q, k, vt, bias)


def _pad_heads(w, heads, dim):
    w = w.reshape(w.shape[0], heads, dim)
    return jnp.pad(w, ((0, 0), (0, 0), (0, LANES - dim))).reshape(w.shape[0], heads * LANES)


def _rot_half_cols(w_rope):
    half = w_rope.shape[-1] // 2
    return jnp.concatenate([-w_rope[..., half:], w_rope[..., :half]], axis=-1)


def _rope_tables(max_len):
    half = MLA_ROPE // 2
    freqs = ROPE_THETA ** (-jnp.arange(half, dtype=F32) / half)
    ang = jnp.arange(max_len, dtype=F32)[:, None] * freqs[None, :]
    cos = jnp.cos(ang)
    sin = jnp.sin(ang)
    pad = jnp.zeros((max_len, LANES - MLA_NOPE - MLA_ROPE), F32)
    cos_t = jnp.concatenate([jnp.ones((max_len, MLA_NOPE), F32), cos, cos, pad], axis=-1)
    sin_t = jnp.concatenate([jnp.zeros((max_len, MLA_NOPE), F32), sin, sin, pad], axis=-1)
    return cos_t, sin_t


def _mla_weights(w_dq, g_q, w_uq, w_dkv, g_kv, w_uk, w_uv, max_len):
    d_qk = MLA_NOPE + MLA_ROPE
    uq = w_uq.reshape(MLA_Q_LORA, MLA_HEADS, d_qk)
    uq_rot = jnp.concatenate(
        [jnp.zeros((MLA_Q_LORA, MLA_HEADS, MLA_NOPE), F32), _rot_half_cols(uq[..., MLA_NOPE:])], axis=-1)
    w_kr = w_dkv[:, MLA_KV_LORA:]
    place = lambda w: jnp.pad(w, ((0, 0), (MLA_NOPE, LANES - d_qk)))
    cos_t, sin_t = _rope_tables(max_len)
    return {
        "wdq": w_dq.astype(BF16),
        "gq": g_q.reshape(1, -1),
        "wuq": _pad_heads(uq.reshape(MLA_Q_LORA, -1), MLA_HEADS, d_qk).astype(BF16),
        "wuq_rot": _pad_heads(uq_rot.reshape(MLA_Q_LORA, -1), MLA_HEADS, d_qk).astype(BF16),
        "wckv": w_dkv[:, :MLA_KV_LORA].astype(BF16),
        "gkv": g_kv.reshape(1, -1),
        "wkr": place(w_kr).astype(BF16),
        "wkr_rot": place(_rot_half_cols(w_kr)).astype(BF16),
        "wuk": _pad_heads(w_uk, MLA_HEADS, MLA_NOPE).astype(BF16),
        "wuvt": w_uv.T.astype(BF16),
        "cos": cos_t,
        "sin": sin_t,
    }


def _ffn_weights(w_gate, w_up, w_down):
    return w_gate.astype(BF16), w_up.astype(BF16), w_down.astype(BF16)


def _diff_vecs(lam_q1, lam_k1, lam_q2, lam_k2, g_sub):
    rows = [jnp.pad(v, (0, LANES - v.shape[0])) for v in (lam_q1, lam_k1, lam_q2, lam_k2)] + [g_sub]
    return jnp.pad(jnp.stack(rows), ((0, VEC_ROWS - len(rows)), (0, 0)))


def _trunk(x, p):
    batch, seq_len, _ = x.shape
    x = x.reshape(batch * seq_len, D_MODEL)
    no_vecs = jnp.zeros((VEC_ROWS, LANES), F32)
    for i in range(DEPTH):
        x = _ffn(x, None, p["norm_g"][i, 0].reshape(1, -1), *p["ffn"][i][0], p["final_g"], False)
        g_mix = p["norm_g"][i, 1].reshape(1, -1)
        m, j = i % N_MIXERS, i // N_MIXERS
        if m == 0:
            w = p["mla"][j]
            q, k, vt = _mla_proj(x, seq_len, g_mix, w)
            a = _dense_attn(q, k, vt, p["t5"], no_vecs, batch, seq_len, False, 0.0)
        elif m == 1:
            w = p["diff"][j]
            lam_init = 0.8 - 0.6 * math.exp(-0.3 * i)
            q, k, vt = _qkv_proj(x, g_mix, w["wq"], w["wk"], w["wvt"], DIFF_QK ** -0.5 * LOG2E, ATT_TK)
            a = _dense_attn(q, k, vt, p["t5"], w["vecs"], batch, seq_len, True, lam_init)
        else:
            w = p["na"][j]
            q, k, vt = _qkv_proj(x, g_mix, w["wq"], w["wk"], w["wvt"], NA_HEAD_DIM ** -0.5 * LOG2E, NA_BLOCK_Q)
            a = _na_attn(q, k, vt, w["bias"], batch, seq_len)
        x = _ffn(x, (a, w["wo"]), p["norm_g"][i, 2].reshape(1, -1), *p["ffn"][i][1], p["final_g"], i == DEPTH - 1)
    return x.reshape(batch, seq_len, D_MODEL)


def kernel(x_prompt, x_sample, norm_g, final_g, ffn_w_gate, ffn_w_up, ffn_w_down, rel_bias_table, mla_w_dq, mla_g_q, mla_w_uq, mla_w_dkv, mla_g_kv, mla_w_uk, mla_w_uv, mla_w_o, diff_w_q, diff_w_k, diff_w_v, diff_lam_q1, diff_lam_k1, diff_lam_q2, diff_lam_k2, diff_g_sub, diff_w_o, na_w_qkv, na_rpb, na_w_o):
    max_len = max(x_prompt.shape[1], x_sample.shape[1])
    hd = NA_HEADS * NA_HEAD_DIM
    p = {
        "norm_g": norm_g,
        "final_g": final_g.reshape(1, -1),
        "ffn": [[_ffn_weights(ffn_w_gate[i, s], ffn_w_up[i, s], ffn_w_down[i, s]) for s in range(2)]
                for i in range(DEPTH)],
        "t5": _t5_tiles(rel_bias_table),
        "mla": [dict(_mla_weights(mla_w_dq[j], mla_g_q[j], mla_w_uq[j], mla_w_dkv[j], mla_g_kv[j],
                                  mla_w_uk[j], mla_w_uv[j], max_len), wo=mla_w_o[j].astype(BF16))
                for j in range(mla_w_dq.shape[0])],
        "diff": [{
            "wq": diff_w_q[j].astype(BF16), "wk": diff_w_k[j].astype(BF16), "wvt": diff_w_v[j].T.astype(BF16),
            "vecs": _diff_vecs(diff_lam_q1[j], diff_lam_k1[j], diff_lam_q2[j], diff_lam_k2[j], diff_g_sub[j]),
            "wo": diff_w_o[j].astype(BF16),
        } for j in range(diff_w_q.shape[0])],
        "na": [{
            "wq": na_w_qkv[j][:, :hd].astype(BF16), "wk": na_w_qkv[j][:, hd:2 * hd].astype(BF16),
            "wvt": na_w_qkv[j][:, 2 * hd:].T.astype(BF16), "bias": _na_bias(na_rpb[j]),
            "wo": na_w_o[j].astype(BF16),
        } for j in range(na_w_qkv.shape[0])],
    }
    return (_trunk(x_prompt, p), _trunk(x_sample, p))
```

```python
import functools
import math

import jax
import jax.numpy as jnp
from jax import lax
from jax.experimental import pallas as pl
from jax.experimental.pallas import tpu as pltpu

F32 = jnp.float32
BF16 = jnp.bfloat16

D_MODEL = 1024
DEPTH = 4
N_MIXERS = 3
GRID_W = 64
RMS_EPS = 1e-6
D_FF = 2816

REL_BUCKETS = 32
N_BIAS_HEADS = 16

MLA_HEADS = 16
MLA_Q_LORA = 512
MLA_KV_LORA = 256
MLA_NOPE = 64
MLA_ROPE = 32
MLA_V = 64
ROPE_THETA = 10000.0

DIFF_HEADS = 8
DIFF_QK = 64
DIFF_V = 2 * DIFF_QK

NA_HEADS = 16
NA_HEAD_DIM = 64
NA_KR = 8
NA_KC = 16

LANES = 128
MXU_WIDTH = 256
V7X_VMEM_BYTES = 64 * 1024 * 1024
VMEM_LIMIT_BYTES = V7X_VMEM_BYTES * 7 // 8

TOK_TILE = 512
FF_CHUNK = MXU_WIDTH
ATT_TQ_MAX = 1024
ATT_STRIP = MXU_WIDTH
FAR_PER_TRIP = 4
VEC_ROWS = 32
ATT_TK = TOK_TILE
BIAS_TILE = LANES
PAIR_W = 2 * NA_HEAD_DIM
NA_BLOCK_ROWS = 4
NA_SLAB_ROWS = NA_BLOCK_ROWS + NA_KR
NA_BLOCK_Q = NA_BLOCK_ROWS * GRID_W
NA_SLAB_Q = NA_SLAB_ROWS * GRID_W
MASK_VALUE = -1e30
ONES_ROWS = 16
LOG2E = math.log2(math.e)

T5_THRESHOLDS = (12, 16, 23, 32, 46, 64, 91)


def _params(*semantics):
    return pltpu.CompilerParams(dimension_semantics=semantics, vmem_limit_bytes=VMEM_LIMIT_BYTES)


def _resident(shape):
    zeros = (0,) * len(shape)
    return pl.BlockSpec(shape, lambda *_: zeros, pipeline_mode=pl.Buffered(1))


def _rms(x, g):
    return x * lax.rsqrt(jnp.mean(x * x, axis=-1, keepdims=True) + RMS_EPS) * g


def _dot(a, b):
    return jnp.dot(a, b, preferred_element_type=F32)


def _dot_nt(a, b):
    return lax.dot_general(a, b, (((1,), (1,)), ((), ())), preferred_element_type=F32)


def _ffn_kernel(*refs, add_mixer, apply_final):
    if add_mixer:
        x_ref, a_ref, wo_ref, g_ref, wg_ref, wu_ref, wd_ref, fg_ref, o_ref = refs
        x = x_ref[...] + _dot(a_ref[...], wo_ref[...])
    else:
        x_ref, g_ref, wg_ref, wu_ref, wd_ref, fg_ref, o_ref = refs
        x = x_ref[...]
    h = _rms(x, g_ref[...]).astype(BF16)
    acc = jnp.zeros(x.shape, F32)
    for c in range(D_FF // FF_CHUNK):
        sl = slice(c * FF_CHUNK, (c + 1) * FF_CHUNK)
        gate = _dot(h, wg_ref[:, sl])
        up = _dot(h, wu_ref[:, sl])
        act = (gate / (1.0 + jnp.exp(-gate))) * up
        acc = acc + _dot(act.astype(BF16), wd_ref[sl, :])
    y = x + 0.5 * acc
    if apply_final:
        y = _rms(y, fg_ref[...])
    o_ref[...] = y


def _ffn(x, mixer, g, wg, wu, wd, final_g, apply_final):
    t = x.shape[0]
    tok = pl.BlockSpec((TOK_TILE, D_MODEL), lambda i: (i, 0))
    mixer_specs = [] if mixer is None else [tok, _resident((D_MODEL, D_MODEL))]
    return pl.pallas_call(
        functools.partial(_ffn_kernel, add_mixer=mixer is not None, apply_final=apply_final),
        out_shape=jax.ShapeDtypeStruct(x.shape, F32),
        grid=(t // TOK_TILE,),
        in_specs=[tok] + mixer_specs + [
            _resident((1, D_MODEL)),
            _resident((D_MODEL, D_FF)),
            _resident((D_MODEL, D_FF)),
            _resident((D_FF, D_MODEL)),
            _resident((1, D_MODEL)),
        ],
        out_specs=tok,
        compiler_params=_params("parallel"),
        name="ffn_mixer" if mixer is not None else "ffn",
    )(x, *(() if mixer is None else mixer), g, wg, wu, wd, final_g)


def _mla_proj_kernel(x_ref, g_ref, wdq_ref, gq_ref, wuq_ref, wuqr_ref, wckv_ref, gkv_ref,
                     wkr_ref, wkrr_ref, wuk_ref, wuvt_ref, cos_ref, sin_ref,
                     q_ref, k_ref, vt_ref):
    hn = _rms(x_ref[...], g_ref[...]).astype(BF16)
    cos = cos_ref[...]
    sin = sin_ref[...]
    scale = (MLA_NOPE + MLA_ROPE) ** -0.5 * LOG2E

    cq = _rms(_dot(hn, wdq_ref[...]), gq_ref[...]).astype(BF16)
    qa = _dot(cq, wuq_ref[...])
    qb = _dot(cq, wuqr_ref[...])
    for h in range(MLA_HEADS):
        sl = slice(h * LANES, (h + 1) * LANES)
        q_ref[:, sl] = ((qa[:, sl] * cos + qb[:, sl] * sin) * scale).astype(BF16)

    ckv = _rms(_dot(hn, wckv_ref[...]), gkv_ref[...]).astype(BF16)
    k_rope = _dot(hn, wkr_ref[...]) * cos + _dot(hn, wkrr_ref[...]) * sin
    k_nope = _dot(ckv, wuk_ref[...])
    for h in range(MLA_HEADS):
        sl = slice(h * LANES, (h + 1) * LANES)
        k_ref[:, sl] = (k_nope[:, sl] + k_rope).astype(BF16)

    vt_ref[0] = _dot_nt(wuvt_ref[...], ckv).astype(BF16)


def _mla_proj(x, seq_len, g, w):
    t = x.shape[0]
    hw = MLA_HEADS * LANES
    pos_blocks = seq_len // TOK_TILE
    tok = lambda width: pl.BlockSpec((TOK_TILE, width), lambda i: (i, 0))
    pos = pl.BlockSpec((TOK_TILE, LANES), lambda i: (i % pos_blocks, 0))
    return pl.pallas_call(
        _mla_proj_kernel,
        out_shape=(
            jax.ShapeDtypeStruct((t, hw), BF16),
            jax.ShapeDtypeStruct((t, hw), BF16),
            jax.ShapeDtypeStruct((t // TOK_TILE, MLA_HEADS * MLA_V, TOK_TILE), BF16),
        ),
        grid=(t // TOK_TILE,),
        in_specs=[
            tok(D_MODEL),
            _resident((1, D_MODEL)),
            _resident((D_MODEL, MLA_Q_LORA)),
            _resident((1, MLA_Q_LORA)),
            _resident((MLA_Q_LORA, hw)),
            _resident((MLA_Q_LORA, hw)),
            _resident((D_MODEL, MLA_KV_LORA)),
            _resident((1, MLA_KV_LORA)),
            _resident((D_MODEL, LANES)),
            _resident((D_MODEL, LANES)),
            _resident((MLA_KV_LORA, hw)),
            _resident((MLA_HEADS * MLA_V, MLA_KV_LORA)),
            pos,
            pos,
        ],
        out_specs=(
            tok(hw),
            tok(hw),
            pl.BlockSpec((1, MLA_HEADS * MLA_V, TOK_TILE), lambda i: (i, 0, 0)),
        ),
        compiler_params=_params("parallel"),
        name="mla_proj",
    )(x, g, w["wdq"], w["gq"], w["wuq"], w["wuq_rot"], w["wckv"], w["gkv"],
      w["wkr"], w["wkr_rot"], w["wuk"], w["wuvt"], w["cos"], w["sin"])


def _qkv_proj_kernel(x_ref, g_ref, wq_ref, wk_ref, wvt_ref, q_ref, k_ref, vt_ref, *, q_scale):
    hn = _rms(x_ref[...], g_ref[...]).astype(BF16)
    q_ref[...] = (_dot(hn, wq_ref[...]) * q_scale).astype(BF16)
    k_ref[...] = _dot(hn, wk_ref[...]).astype(BF16)
    vt = _dot_nt(wvt_ref[...], hn).astype(BF16)
    chunk = vt_ref.shape[-1]
    for c in range(vt_ref.shape[0]):
        vt_ref[c] = vt[:, c * chunk:(c + 1) * chunk]


def _qkv_proj(x, g, wq, wk, wvt, q_scale, v_chunk):
    t = x.shape[0]
    per_tile = TOK_TILE // v_chunk
    tok = pl.BlockSpec((TOK_TILE, D_MODEL), lambda i: (i, 0))
    return pl.pallas_call(
        functools.partial(_qkv_proj_kernel, q_scale=q_scale),
        out_shape=(jax.ShapeDtypeStruct((t, D_MODEL), BF16), jax.ShapeDtypeStruct((t, D_MODEL), BF16),
                   jax.ShapeDtypeStruct((t // v_chunk, D_MODEL, v_chunk), BF16)),
        grid=(t // TOK_TILE,),
        in_specs=[tok, _resident((1, D_MODEL)), _resident((D_MODEL, D_MODEL)),
                  _resident((D_MODEL, D_MODEL)), _resident((D_MODEL, D_MODEL))],
        out_specs=(tok, tok, pl.BlockSpec((per_tile, D_MODEL, v_chunk), lambda i: (i, 0, 0))),
        compiler_params=_params("parallel"),
        name="qkv_proj",
    )(x, g, wq, wk, wvt)


def _t5_tiles_kernel(tab_ref, o_ref):
    head = pl.program_id(0)
    kk = lax.broadcasted_iota(jnp.int32, (BIAS_TILE, BIAS_TILE), 0)
    qq = lax.broadcasted_iota(jnp.int32, (BIAS_TILE, BIAS_TILE), 1)
    half = REL_BUCKETS // 2
    max_exact = half // 2
    for d in range(5):
        rel = (d - 2) * BIAS_TILE + kk - qq
        n = jnp.where(rel < 0, -rel, rel)
        big = jnp.full(rel.shape, max_exact, jnp.int32)
        for thr in T5_THRESHOLDS:
            big = big + jnp.where(n >= thr, 1, 0)
        bucket = jnp.where(rel > 0, half, 0) + jnp.where(n < max_exact, n, big)
        out = jnp.zeros(rel.shape, F32)
        for b in range(REL_BUCKETS):
            out = jnp.where(bucket == b, tab_ref[b, head], out)
        o_ref[d, 0] = out * LOG2E


def _t5_tiles(table):
    return pl.pallas_call(
        _t5_tiles_kernel,
        out_shape=jax.ShapeDtypeStruct((5, N_BIAS_HEADS, BIAS_TILE, BIAS_TILE), F32),
        grid=(N_BIAS_HEADS,),
        in_specs=[pl.BlockSpec(memory_space=pltpu.SMEM)],
        out_specs=pl.BlockSpec((5, 1, BIAS_TILE, BIAS_TILE), lambda h: (0, h, 0, 0)),
        compiler_params=_params("parallel"),
        name="t5_tiles",
    )(table)


def _na_block_delta(block_type, key_row, query_row):
    if block_type == 0:
        return key_row - query_row if key_row < NA_KR else None
    if block_type == 1:
        delta = key_row - query_row - NA_KR // 2
        return delta if -(NA_KR // 2) <= delta < NA_KR // 2 else None
    return key_row - query_row - NA_KR if key_row >= NA_SLAB_ROWS - NA_KR else None


def _na_bias_kernel(rpb_ref, o_ref):
    head = pl.program_id(0)
    n_col = 2 * NA_KC - 1
    shape = (GRID_W, NA_BLOCK_Q)
    kc = lax.broadcasted_iota(jnp.int32, shape, 0)
    col = lax.broadcasted_iota(jnp.int32, shape, 1)
    c = col & (GRID_W - 1)
    block_row = col >> int(math.log2(GRID_W))
    start = jnp.clip(c - NA_KC // 2, 0, GRID_W - NA_KC)
    valid = (kc >= start) & (kc < start + NA_KC)
    col_off = kc - c + (NA_KC - 1)
    masked = jnp.full(shape, MASK_VALUE, F32)
    per_row_offset = []
    for ro in range(2 * NA_KR - 1):
        out = masked
        for co in range(n_col):
            out = jnp.where(valid & (col_off == co), rpb_ref[head, ro * n_col + co] * LOG2E, out)
        per_row_offset.append(out)
    for block_type in range(3):
        for key_row in range(NA_SLAB_ROWS):
            piece = masked
            for query_row in range(NA_BLOCK_ROWS):
                delta = _na_block_delta(block_type, key_row, query_row)
                if delta is not None:
                    piece = jnp.where(block_row == query_row, per_row_offset[delta + NA_KR - 1], piece)
            o_ref[0, block_type, key_row * GRID_W:(key_row + 1) * GRID_W, :] = piece


def _na_bias(rpb):
    n_row = 2 * NA_KR - 1
    n_col = 2 * NA_KC - 1
    return pl.pallas_call(
        _na_bias_kernel,
        out_shape=jax.ShapeDtypeStruct((NA_HEADS, 3, NA_SLAB_Q, NA_BLOCK_Q), F32),
        grid=(NA_HEADS,),
        in_specs=[pl.BlockSpec(memory_space=pltpu.SMEM)],
        out_specs=pl.BlockSpec((1, 3, NA_SLAB_Q, NA_BLOCK_Q), lambda h: (h, 0, 0, 0)),
        compiler_params=_params("parallel"),
        name="na_bias",
    )(rpb.reshape(NA_HEADS, n_row * n_col))


def _dense_attn_kernel(q_ref, k_ref, vt_ref, bias_ref, vec_ref, o_ref, qm_ref, sa_ref, sb_ref, stat_ref, acc_ref,
                       *, diff, lam_init, n_qb):
    qi = pl.program_id(2)
    n_kb = vt_ref.shape[0]
    tq = q_ref.shape[0]
    sub_k = ATT_TK // BIAS_TILE
    dv = DIFF_V if diff else MLA_V
    run_max, max_a, max_b = 0, 2, 4

    q = q_ref[...]
    if diff:
        lane = lax.broadcasted_iota(jnp.int32, q.shape, 1)
        zero = jnp.zeros_like(q)
        qm_ref[0] = jnp.where(lane < DIFF_QK, q, zero).T
        qm_ref[1] = jnp.where(lane >= DIFF_QK, q, zero).T
    else:
        qm_ref[0] = q[:, :LANES].T
        qm_ref[1] = q[:, LANES:].T
    stat_ref[...] = jnp.full(stat_ref.shape, MASK_VALUE, F32)
    acc_ref[...] = jnp.zeros(acc_ref.shape, F32)
    ones = jnp.ones((ONES_ROWS, ATT_TK), BF16)

    tile_rows = lambda a: slice(a * BIAS_TILE, (a + 1) * BIAS_TILE)
    side_const = lambda t: (bias_ref[0, t, 0:1, 0:1], bias_ref[4, t, 0:1, 0:1])

    def scores(j, buf, off):
        s_ref, mb_row = buf
        kb = k_ref[pl.ds(pl.multiple_of(j * ATT_TK, ATT_TK), ATT_TK), :]
        for t in range(2):
            kt = kb if diff else kb[:, t * LANES:(t + 1) * LANES]
            c_left, c_right = side_const(t)
            for st in range(tq // ATT_STRIP):
                cols = slice(st * ATT_STRIP, (st + 1) * ATT_STRIP)
                s = _dot(kt, qm_ref[t, :, cols])
                if off is None:
                    s_ref[t, :, cols] = s
                    stat_ref[mb_row + t, :, cols] = jnp.max(s, axis=0, keepdims=True)
                    continue
                for bb in range(ATT_STRIP // BIAS_TILE):
                    b = st * (ATT_STRIP // BIAS_TILE) + bb
                    best = None
                    for a in range(sub_k):
                        piece = s[tile_rows(a), tile_rows(bb)]
                        o = off(a, b)
                        if abs(o) <= 1:
                            piece = piece + bias_ref[o + 2, t]
                            piece_max = jnp.max(piece, axis=0, keepdims=True)
                        else:
                            piece_max = jnp.max(piece, axis=0, keepdims=True) + (c_left if o < 0 else c_right)
                        s_ref[t, tile_rows(a), tile_rows(b)] = piece
                        best = piece_max if best is None else jnp.maximum(best, piece_max)
                    stat_ref[mb_row + t, :, tile_rows(b)] = best

    def update(j, buf, left, off):
        s_ref, mb_row = buf
        vb = vt_ref[j]
        for t in range(2):
            c_left, c_right = side_const(t)
            vv = vb if diff else vb[t * MLA_V:(t + 1) * MLA_V]
            lhs = jnp.concatenate([vv, ones], axis=0)
            for st in range(tq // ATT_STRIP):
                cols = slice(st * ATT_STRIP, (st + 1) * ATT_STRIP)
                m_prev = stat_ref[run_max + t, :, cols]
                if off is None:
                    c = jnp.where(left, c_left, c_right)
                    m_new = jnp.maximum(m_prev, stat_ref[mb_row + t, :, cols] + c)
                    p = jnp.exp2(s_ref[t, :, cols] - (m_new - c)).astype(BF16)
                else:
                    m_new = jnp.maximum(m_prev, stat_ref[mb_row + t, :, cols])
                    p_cols = []
                    for bb in range(ATT_STRIP // BIAS_TILE):
                        b = st * (ATT_STRIP // BIAS_TILE) + bb
                        m_b = m_new[:, tile_rows(bb)]
                        shift = {-1: m_b - c_left, 0: m_b, 1: m_b - c_right}
                        p_cols.append(jnp.concatenate(
                            [jnp.exp2(s_ref[t, tile_rows(a), tile_rows(b)]
                                      - shift[0 if abs(off(a, b)) <= 1 else (1 if off(a, b) > 0 else -1)])
                             for a in range(sub_k)], axis=0))
                    p = jnp.concatenate(p_cols, axis=1).astype(BF16)
                alpha = jnp.exp2(m_prev - m_new)
                acc_ref[t, :, cols] = acc_ref[t, :, cols] * alpha + _dot(lhs, p)
                stat_ref[run_max + t, :, cols] = m_new

    bufs = ((sa_ref, max_a), (sb_ref, max_b))
    n_near = tq // ATT_TK + 2
    n_far = n_kb - n_near
    w0 = jnp.clip(qi * (tq // ATT_TK) - 1, 0, n_far)
    far_block = lambda i: jnp.where(i < w0, i, i + n_near)
    far_buf = lambda parity: bufs[(n_near + parity) % 2]

    def near_phase(delta):
        def tile_offset(i):
            return lambda a, b: (i + delta) * sub_k + a - b

        def near_step(i):
            if i < n_near:
                scores(w0 + i, bufs[i % 2], tile_offset(i))
            elif n_far:
                scores(far_block(0), far_buf(0), None)
            if i > 0:
                update(w0 + i - 1, bufs[(i - 1) % 2], None, tile_offset(i - 1))

        one_trip = jnp.minimum(qi + 1, 1)

        def region(first, last):
            def body(_, carry):
                for i in range(first, last):
                    near_step(i)
                return carry
            lax.fori_loop(0, one_trip, body, 0)

        region(0, 3)
        for first in range(3, n_near + 1, 2):
            region(first, min(first + 2, n_near + 1))

    pl.when(qi == 0)(lambda: near_phase(0))
    pl.when(qi == n_qb - 1)(lambda: near_phase(-2))
    if n_qb > 2:
        pl.when((qi > 0) & (qi < n_qb - 1))(lambda: near_phase(-1))
    if n_far:

        def far_step(i, parity):
            j_prev = far_block(i - 1)
            scores(far_block(i), far_buf(parity), None)
            update(j_prev, far_buf(1 - parity), j_prev < w0, None)

        def far_trip(trip, carry):
            for u in range(1, FAR_PER_TRIP + 1):
                far_step(FAR_PER_TRIP * trip + u, u % 2)
            return carry

        n_trips = (n_far - 1) // FAR_PER_TRIP
        lax.fori_loop(0, n_trips, far_trip, 0)
        for i in range(FAR_PER_TRIP * n_trips + 1, n_far):
            far_step(i, i % 2)
        j_last = far_block(n_far - 1)
        update(j_last, far_buf((n_far - 1) % 2), j_last < w0, None)

    a0 = acc_ref[0]
    a1 = acc_ref[1]
    o0 = a0[:dv] / a0[dv:dv + 1]
    o1 = a1[:dv] / a1[dv:dv + 1]
    if diff:
        lam_dot = lambda r: jnp.sum(vec_ref[r:r + 1, :DIFF_QK] * vec_ref[r + 1:r + 2, :DIFF_QK], axis=-1, keepdims=True)
        lam = jnp.exp(lam_dot(0)) - jnp.exp(lam_dot(2)) + lam_init
        o = (o0 - lam * o1).T
        o = _rms(o, vec_ref[4:5, :]) * (1.0 - lam_init)
    else:
        o = jnp.concatenate([o0, o1], axis=0).T
    o_ref[...] = o.astype(BF16)


def _dense_attn(q, k, vt, bias, vecs, batch, seq_len, diff, lam_init):
    pw = PAIR_W if diff else 2 * LANES
    dv = DIFF_V if diff else MLA_V
    tq = min(ATT_TQ_MAX, seq_len // 2)
    nq = seq_len // tq
    nk = seq_len // ATT_TK
    assert tq % ATT_TK == 0 and seq_len % tq == 0 and nk >= tq // ATT_TK + 2, "the near window must fit"
    return pl.pallas_call(
        functools.partial(_dense_attn_kernel, diff=diff, lam_init=lam_init, n_qb=nq),
        out_shape=jax.ShapeDtypeStruct((batch * seq_len, D_MODEL), BF16),
        grid=(batch, N_BIAS_HEADS // 2, nq),
        in_specs=[
            pl.BlockSpec((tq, pw), lambda b, h, i: (b * nq + i, h)),
            pl.BlockSpec((seq_len, pw), lambda b, h, i: (b, h)),
            pl.BlockSpec((nk, PAIR_W, ATT_TK), lambda b, h, i: (b, h, 0)),
            pl.BlockSpec((5, 2, BIAS_TILE, BIAS_TILE), lambda b, h, i: (0, h, 0, 0)),
            pl.BlockSpec(vecs.shape, lambda b, h, i: (0, 0)),
        ],
        out_specs=pl.BlockSpec((tq, PAIR_W), lambda b, h, i: (b * nq + i, h)),
        scratch_shapes=[
            pltpu.VMEM((2, LANES, tq), BF16),
            pltpu.VMEM((2, ATT_TK, tq), F32),
            pltpu.VMEM((2, ATT_TK, tq), F32),
            pltpu.VMEM((8, 1, tq), F32),
            pltpu.VMEM((2, dv + ONES_ROWS, tq), F32),
        ],
        compiler_params=_params("parallel", "parallel", "parallel"),
        name="diff_attn" if diff else "mla_attn",
    )(q, k, vt, bias, vecs)


def _na_attn_kernel(q_ref, k_ref, vt_ref, bias_ref, o_ref, sa_ref, sb_ref, stat_ref):
    n_blocks = q_ref.shape[0] // NA_BLOCK_Q
    chunks = NA_SLAB_Q // NA_BLOCK_Q
    lane = lax.broadcasted_iota(jnp.int32, (NA_BLOCK_Q, PAIR_W), 1)
    first = lane < NA_HEAD_DIM
    ones = jnp.ones((ONES_ROWS, NA_SLAB_Q), BF16)
    slab_chunk = lambda g: jnp.clip(g - 1, 0, n_blocks - chunks)

    def scores(g, buf):
        s_ref, mb_row = buf
        block_type = jnp.where(g == 0, 0, jnp.where(g == n_blocks - 1, 2, 1))
        q = q_ref[pl.ds(pl.multiple_of(g * NA_BLOCK_Q, NA_BLOCK_Q), NA_BLOCK_Q), :]
        ks = k_ref[pl.ds(pl.multiple_of(slab_chunk(g) * NA_BLOCK_Q, NA_BLOCK_Q), NA_SLAB_Q), :]
        zero = jnp.zeros_like(q)
        for t in range(2):
            qt = jnp.where(first if t == 0 else ~first, q, zero)
            s = _dot_nt(ks, qt) + bias_ref[t, block_type]
            s_ref[t] = s
            stat_ref[mb_row + t] = jnp.max(s, axis=0, keepdims=True)

    def finish(g, buf):
        s_ref, mb_row = buf
        c0 = slab_chunk(g)
        vts = jnp.concatenate([vt_ref[c0 + i] for i in range(chunks)], axis=1)
        outs = []
        for t in range(2):
            p = jnp.exp2(s_ref[t] - stat_ref[mb_row + t]).astype(BF16)
            vv = vts[t * NA_HEAD_DIM:(t + 1) * NA_HEAD_DIM]
            acc = _dot(jnp.concatenate([vv, ones], axis=0), p)
            outs.append(acc[:NA_HEAD_DIM] / acc[NA_HEAD_DIM:NA_HEAD_DIM + 1])
        o = jnp.concatenate(outs, axis=0).T
        o_ref[pl.ds(pl.multiple_of(g * NA_BLOCK_Q, NA_BLOCK_Q), NA_BLOCK_Q), :] = o.astype(BF16)

    buf_a = (sa_ref, 0)
    buf_b = (sb_ref, 2)
    scores(0, buf_a)

    def two_blocks(pair, carry):
        g = 2 * pair
        scores(g + 1, buf_b)
        finish(g, buf_a)
        scores(g + 2, buf_a)
        finish(g + 1, buf_b)
        return carry

    lax.fori_loop(0, n_blocks // 2 - 1, two_blocks, 0)
    scores(n_blocks - 1, buf_b)
    finish(n_blocks - 2, buf_a)
    finish(n_blocks - 1, buf_b)


def _na_attn(q, k, vt, bias, batch, seq_len):
    n_blocks = seq_len // NA_BLOCK_Q
    assert n_blocks % 2 == 0 and n_blocks * NA_BLOCK_ROWS >= NA_SLAB_ROWS
    seq = pl.BlockSpec((seq_len, PAIR_W), lambda h, b: (b, h))
    return pl.pallas_call(
        _na_attn_kernel,
        out_shape=jax.ShapeDtypeStruct((batch * seq_len, D_MODEL), BF16),
        grid=(NA_HEADS // 2, batch),
        in_specs=[seq, seq,
                  pl.BlockSpec((n_blocks, PAIR_W, NA_BLOCK_Q), lambda h, b: (b, h, 0)),
                  pl.BlockSpec((2, 3, NA_SLAB_Q, NA_BLOCK_Q), lambda h, b: (h, 0, 0, 0))],
        out_specs=seq,
        scratch_shapes=[
            pltpu.VMEM((2, NA_SLAB_Q, NA_BLOCK_Q), F32),
            pltpu.VMEM((2, NA_SLAB_Q, NA_BLOCK_Q), F32),
            pltpu.VMEM((16, 1, NA_BLOCK_Q), F32),
        ],
        compiler_params=_params("parallel", "parallel"),
        name="na_attn",
    )(q, k, vt, bias)


def _pad_heads(w, heads, dim):
    w = w.reshape(w.shape[0], heads, dim)
    return jnp.pad(w, ((0, 0), (0, 0), (0, LANES - dim))).reshape(w.shape[0], heads * LANES)


def _rot_half_cols(w_rope):
    half = w_rope.shape[-1] // 2
    return jnp.concatenate([-w_rope[..., half:], w_rope[..., :half]], axis=-1)


def _rope_tables(max_len):
    half = MLA_ROPE // 2
    freqs = ROPE_THETA ** (-jnp.arange(half, dtype=F32) / half)
    ang = jnp.arange(max_len, dtype=F32)[:, None] * freqs[None, :]
    cos = jnp.cos(ang)
    sin = jnp.sin(ang)
    pad = jnp.zeros((max_len, LANES - MLA_NOPE - MLA_ROPE), F32)
    cos_t = jnp.concatenate([jnp.ones((max_len, MLA_NOPE), F32), cos, cos, pad], axis=-1)
    sin_t = jnp.concatenate([jnp.zeros((max_len, MLA_NOPE), F32), sin, sin, pad], axis=-1)
    return cos_t, sin_t


def _mla_weights(w_dq, g_q, w_uq, w_dkv, g_kv, w_uk, w_uv, max_len):
    d_qk = MLA_NOPE + MLA_ROPE
    uq = w_uq.reshape(MLA_Q_LORA, MLA_HEADS, d_qk)
    uq_rot = jnp.concatenate(
        [jnp.zeros((MLA_Q_LORA, MLA_HEADS, MLA_NOPE), F32), _rot_half_cols(uq[..., MLA_NOPE:])], axis=-1)
    w_kr = w_dkv[:, MLA_KV_LORA:]
    place = lambda w: jnp.pad(w, ((0, 0), (MLA_NOPE, LANES - d_qk)))
    cos_t, sin_t = _rope_tables(max_len)
    return {
        "wdq": w_dq.astype(BF16),
        "gq": g_q.reshape(1, -1),
        "wuq": _pad_heads(uq.reshape(MLA_Q_LORA, -1), MLA_HEADS, d_qk).astype(BF16),
        "wuq_rot": _pad_heads(uq_rot.reshape(MLA_Q_LORA, -1), MLA_HEADS, d_qk).astype(BF16),
        "wckv": w_dkv[:, :MLA_KV_LORA].astype(BF16),
        "gkv": g_kv.reshape(1, -1),
        "wkr": place(w_kr).astype(BF16),
        "wkr_rot": place(_rot_half_cols(w_kr)).astype(BF16),
        "wuk": _pad_heads(w_uk, MLA_HEADS, MLA_NOPE).astype(BF16),
        "wuvt": w_uv.T.astype(BF16),
        "cos": cos_t,
        "sin": sin_t,
    }


def _ffn_weights(w_gate, w_up, w_down):
    return w_gate.astype(BF16), w_up.astype(BF16), w_down.astype(BF16)


def _diff_vecs(lam_q1, lam_k1, lam_q2, lam_k2, g_sub):
    rows = [jnp.pad(v, (0, LANES - v.shape[0])) for v in (lam_q1, lam_k1, lam_q2, lam_k2)] + [g_sub]
    return jnp.pad(jnp.stack(rows), ((0, VEC_ROWS - len(rows)), (0, 0)))


def _trunk(x, p):
    batch, seq_len, _ = x.shape
    x = x.reshape(batch * seq_len, D_MODEL)
    no_vecs = jnp.zeros((VEC_ROWS, LANES), F32)
    for i in range(DEPTH):
        x = _ffn(x, None, p["norm_g"][i, 0].reshape(1, -1), *p["ffn"][i][0], p["final_g"], False)
        g_mix = p["norm_g"][i, 1].reshape(1, -1)
        m, j = i % N_MIXERS, i // N_MIXERS
        if m == 0:
            w = p["mla"][j]
            q, k, vt = _mla_proj(x, seq_len, g_mix, w)
            a = _dense_attn(q, k, vt, p["t5"], no_vecs, batch, seq_len, False, 0.0)
        elif m == 1:
            w = p["diff"][j]
            lam_init = 0.8 - 0.6 * math.exp(-0.3 * i)
            q, k, vt = _qkv_proj(x, g_mix, w["wq"], w["wk"], w["wvt"], DIFF_QK ** -0.5 * LOG2E, ATT_TK)
            a = _dense_attn(q, k, vt, p["t5"], w["vecs"], batch, seq_len, True, lam_init)
        else:
            w = p["na"][j]
            q, k, vt = _qkv_proj(x, g_mix, w["wq"], w["wk"], w["wvt"], NA_HEAD_DIM ** -0.5 * LOG2E, NA_BLOCK_Q)
            a = _na_attn(q, k, vt, w["bias"], batch, seq_len)
        x = _ffn(x, (a, w["wo"]), p["norm_g"][i, 2].reshape(1, -1), *p["ffn"][i][1], p["final_g"], i == DEPTH - 1)
    return x.reshape(batch, seq_len, D_MODEL)


def kernel(x_prompt, x_sample, norm_g, final_g, ffn_w_gate, ffn_w_up, ffn_w_down, rel_bias_table, mla_w_dq, mla_g_q, mla_w_uq, mla_w_dkv, mla_g_kv, mla_w_uk, mla_w_uv, mla_w_o, diff_w_q, diff_w_k, diff_w_v, diff_lam_q1, diff_lam_k1, diff_lam_q2, diff_lam_k2, diff_g_sub, diff_w_o, na_w_qkv, na_rpb, na_w_o):
    max_len = max(x_prompt.shape[1], x_sample.shape[1])
    hd = NA_HEADS * NA_HEAD_DIM
    p = {
        "norm_g": norm_g,
        "final_g": final_g.reshape(1, -1),
        "ffn": [[_ffn_weights(ffn_w_gate[i, s], ffn_w_up[i, s], ffn_w_down[i, s]) for s in range(2)]
                for i in range(DEPTH)],
        "t5": _t5_tiles(rel_bias_table),
        "mla": [dict(_mla_weights(mla_w_dq[j], mla_g_q[j], mla_w_uq[j], mla_w_dkv[j], mla_g_kv[j],
                                  mla_w_uk[j], mla_w_uv[j], max_len), wo=mla_w_o[j].astype(BF16))
                for j in range(mla_w_dq.shape[0])],
        "diff": [{
            "wq": diff_w_q[j].astype(BF16), "wk": diff_w_k[j].astype(BF16), "wvt": diff_w_v[j].T.astype(BF16),
            "vecs": _diff_vecs(diff_lam_q1[j], diff_lam_k1[j], diff_lam_q2[j], diff_lam_k2[j], diff_g_sub[j]),
            "wo": diff_w_o[j].astype(BF16),
        } for j in range(diff_w_q.shape[0])],
        "na": [{
            "wq": na_w_qkv[j][:, :hd].astype(BF16), "wk": na_w_qkv[j][:, hd:2 * hd].astype(BF16),
            "wvt": na_w_qkv[j][:, 2 * hd:].T.astype(BF16), "bias": _na_bias(na_rpb[j]),
            "wo": na_w_o[j].astype(BF16),
        } for j in range(na_w_qkv.shape[0])],
    }
    return (_trunk(x_prompt, p), _trunk(x_sample, p))
```

```python
import functools
import math

import jax
import jax.numpy as jnp
from jax import lax
from jax.experimental import pallas as pl
from jax.experimental.pallas import tpu as pltpu

F32 = jnp.float32
BF16 = jnp.bfloat16

D_MODEL = 1024
DEPTH = 4
N_MIXERS = 3
GRID_W = 64
RMS_EPS = 1e-6
D_FF = 2816

REL_BUCKETS = 32
N_BIAS_HEADS = 16

MLA_HEADS = 16
MLA_Q_LORA = 512
MLA_KV_LORA = 256
MLA_NOPE = 64
MLA_ROPE = 32
MLA_V = 64
ROPE_THETA = 10000.0

DIFF_HEADS = 8
DIFF_QK = 64
DIFF_V = 2 * DIFF_QK

NA_HEADS = 16
NA_HEAD_DIM = 64
NA_KR = 8
NA_KC = 16

LANES = 128
MXU_WIDTH = 256
V7X_VMEM_BYTES = 64 * 1024 * 1024
VMEM_LIMIT_BYTES = V7X_VMEM_BYTES * 7 // 8

TOK_TILE = 512
FF_CHUNK = MXU_WIDTH
ATT_TQ_MAX = 1024
ATT_STRIP = MXU_WIDTH
FAR_PER_TRIP = 4
VEC_ROWS = 32
ATT_TK = TOK_TILE
BIAS_TILE = LANES
PAIR_W = 2 * NA_HEAD_DIM
NA_BLOCK_ROWS = 4
NA_SLAB_ROWS = NA_BLOCK_ROWS + NA_KR
NA_BLOCK_Q = NA_BLOCK_ROWS * GRID_W
NA_SLAB_Q = NA_SLAB_ROWS * GRID_W
MASK_VALUE = -1e30
ONES_ROWS = 16
LOG2E = math.log2(math.e)

T5_THRESHOLDS = (12, 16, 23, 32, 46, 64, 91)


def _params(*semantics):
    return pltpu.CompilerParams(dimension_semantics=semantics, vmem_limit_bytes=VMEM_LIMIT_BYTES)


def _resident(shape):
    zeros = (0,) * len(shape)
    return pl.BlockSpec(shape, lambda *_: zeros, pipeline_mode=pl.Buffered(1))


def _rms(x, g):
    return x * lax.rsqrt(jnp.mean(x * x, axis=-1, keepdims=True) + RMS_EPS) * g


def _dot(a, b):
    return jnp.dot(a, b, preferred_element_type=F32)


def _dot_nt(a, b):
    return lax.dot_general(a, b, (((1,), (1,)), ((), ())), preferred_element_type=F32)


def _ffn_kernel(*refs, add_mixer, apply_final):
    if add_mixer:
        x_ref, a_ref, wo_ref, g_ref, wg_ref, wu_ref, wd_ref, fg_ref, o_ref, act_ref = refs
        x = x_ref[...] + _dot(a_ref[...], wo_ref[...])
    else:
        x_ref, g_ref, wg_ref, wu_ref, wd_ref, fg_ref, o_ref, act_ref = refs
        x = x_ref[...]
    h = _rms(x, g_ref[...]).astype(BF16)
    for c in range(D_FF // FF_CHUNK):
        sl = slice(c * FF_CHUNK, (c + 1) * FF_CHUNK)
        gate = _dot(h, wg_ref[:, sl])
        up = _dot(h, wu_ref[:, sl])
        act_ref[:, sl] = ((gate / (1.0 + jnp.exp(-gate))) * up).astype(BF16)
    y = x + 0.5 * _dot(act_ref[...], wd_ref[...])
    if apply_final:
        y = _rms(y, fg_ref[...])
    o_ref[...] = y


def _ffn(x, mixer, g, wg, wu, wd, final_g, apply_final):
    t = x.shape[0]
    tok = pl.BlockSpec((TOK_TILE, D_MODEL), lambda i: (i, 0))
    mixer_specs = [] if mixer is None else [tok, _resident((D_MODEL, D_MODEL))]
    return pl.pallas_call(
        functools.partial(_ffn_kernel, add_mixer=mixer is not None, apply_final=apply_final),
        out_shape=jax.ShapeDtypeStruct(x.shape, F32),
        grid=(t // TOK_TILE,),
        in_specs=[tok] + mixer_specs + [
            _resident((1, D_MODEL)),
            _resident((D_MODEL, D_FF)),
            _resident((D_MODEL, D_FF)),
            _resident((D_FF, D_MODEL)),
            _resident((1, D_MODEL)),
        ],
        out_specs=tok,
        scratch_shapes=[pltpu.VMEM((TOK_TILE, D_FF), BF16)],
        compiler_params=_params("parallel"),
        name="ffn_mixer" if mixer is not None else "ffn",
    )(x, *(() if mixer is None else mixer), g, wg, wu, wd, final_g)


def _mla_proj_kernel(x_ref, g_ref, wdq_ref, gq_ref, wuq_ref, wuqr_ref, wckv_ref, gkv_ref,
                     wkr_ref, wkrr_ref, wuk_ref, wuvt_ref, cos_ref, sin_ref,
                     q_ref, k_ref, vt_ref):
    hn = _rms(x_ref[...], g_ref[...]).astype(BF16)
    cos = cos_ref[...]
    sin = sin_ref[...]
    scale = (MLA_NOPE + MLA_ROPE) ** -0.5 * LOG2E

    cq = _rms(_dot(hn, wdq_ref[...]), gq_ref[...]).astype(BF16)
    qa = _dot(cq, wuq_ref[...])
    qb = _dot(cq, wuqr_ref[...])
    for h in range(MLA_HEADS):
        sl = slice(h * LANES, (h + 1) * LANES)
        q_ref[:, sl] = ((qa[:, sl] * cos + qb[:, sl] * sin) * scale).astype(BF16)

    ckv = _rms(_dot(hn, wckv_ref[...]), gkv_ref[...]).astype(BF16)
    k_rope = _dot(hn, wkr_ref[...]) * cos + _dot(hn, wkrr_ref[...]) * sin
    k_nope = _dot(ckv, wuk_ref[...])
    for h in range(MLA_HEADS):
        sl = slice(h * LANES, (h + 1) * LANES)
        k_ref[:, sl] = (k_nope[:, sl] + k_rope).astype(BF16)

    vt_ref[0] = _dot_nt(wuvt_ref[...], ckv).astype(BF16)


def _mla_proj(x, seq_len, g, w):
    t = x.shape[0]
    hw = MLA_HEADS * LANES
    pos_blocks = seq_len // TOK_TILE
    tok = lambda width: pl.BlockSpec((TOK_TILE, width), lambda i: (i, 0))
    pos = pl.BlockSpec((TOK_TILE, LANES), lambda i: (i % pos_blocks, 0))
    return pl.pallas_call(
        _mla_proj_kernel,
        out_shape=(
            jax.ShapeDtypeStruct((t, hw), BF16),
            jax.ShapeDtypeStruct((t, hw), BF16),
            jax.ShapeDtypeStruct((t // TOK_TILE, MLA_HEADS * MLA_V, TOK_TILE), BF16),
        ),
        grid=(t // TOK_TILE,),
        in_specs=[
            tok(D_MODEL),
            _resident((1, D_MODEL)),
            _resident((D_MODEL, MLA_Q_LORA)),
            _resident((1, MLA_Q_LORA)),
            _resident((MLA_Q_LORA, hw)),
            _resident((MLA_Q_LORA, hw)),
            _resident((D_MODEL, MLA_KV_LORA)),
            _resident((1, MLA_KV_LORA)),
            _resident((D_MODEL, LANES)),
            _resident((D_MODEL, LANES)),
            _resident((MLA_KV_LORA, hw)),
            _resident((MLA_HEADS * MLA_V, MLA_KV_LORA)),
            pos,
            pos,
        ],
        out_specs=(
            tok(hw),
            tok(hw),
            pl.BlockSpec((1, MLA_HEADS * MLA_V, TOK_TILE), lambda i: (i, 0, 0)),
        ),
        compiler_params=_params("parallel"),
        name="mla_proj",
    )(x, g, w["wdq"], w["gq"], w["wuq"], w["wuq_rot"], w["wckv"], w["gkv"],
      w["wkr"], w["wkr_rot"], w["wuk"], w["wuvt"], w["cos"], w["sin"])


def _qkv_proj_kernel(x_ref, g_ref, wq_ref, wk_ref, wvt_ref, q_ref, k_ref, vt_ref, *, q_scale):
    hn = _rms(x_ref[...], g_ref[...]).astype(BF16)
    q_ref[...] = (_dot(hn, wq_ref[...]) * q_scale).astype(BF16)
    k_ref[...] = _dot(hn, wk_ref[...]).astype(BF16)
    vt = _dot_nt(wvt_ref[...], hn).astype(BF16)
    chunk = vt_ref.shape[-1]
    for c in range(vt_ref.shape[0]):
        vt_ref[c] = vt[:, c * chunk:(c + 1) * chunk]


def _qkv_proj(x, g, wq, wk, wvt, q_scale, v_chunk):
    t = x.shape[0]
    per_tile = TOK_TILE // v_chunk
    tok = pl.BlockSpec((TOK_TILE, D_MODEL), lambda i: (i, 0))
    return pl.pallas_call(
        functools.partial(_qkv_proj_kernel, q_scale=q_scale),
        out_shape=(jax.ShapeDtypeStruct((t, D_MODEL), BF16), jax.ShapeDtypeStruct((t, D_MODEL), BF16),
                   jax.ShapeDtypeStruct((t // v_chunk, D_MODEL, v_chunk), BF16)),
        grid=(t // TOK_TILE,),
        in_specs=[tok, _resident((1, D_MODEL)), _resident((D_MODEL, D_MODEL)),
                  _resident((D_MODEL, D_MODEL)), _resident((D_MODEL, D_MODEL))],
        out_specs=(tok, tok, pl.BlockSpec((per_tile, D_MODEL, v_chunk), lambda i: (i, 0, 0))),
        compiler_params=_params("parallel"),
        name="qkv_proj",
    )(x, g, wq, wk, wvt)


def _t5_tiles_kernel(tab_ref, o_ref):
    head = pl.program_id(0)
    kk = lax.broadcasted_iota(jnp.int32, (BIAS_TILE, BIAS_TILE), 0)
    qq = lax.broadcasted_iota(jnp.int32, (BIAS_TILE, BIAS_TILE), 1)
    half = REL_BUCKETS // 2
    max_exact = half // 2
    for d in range(5):
        rel = (d - 2) * BIAS_TILE + kk - qq
        n = jnp.where(rel < 0, -rel, rel)
        big = jnp.full(rel.shape, max_exact, jnp.int32)
        for thr in T5_THRESHOLDS:
            big = big + jnp.where(n >= thr, 1, 0)
        bucket = jnp.where(rel > 0, half, 0) + jnp.where(n < max_exact, n, big)
        out = jnp.zeros(rel.shape, F32)
        for b in range(REL_BUCKETS):
            out = jnp.where(bucket == b, tab_ref[b, head], out)
        o_ref[d, 0] = out * LOG2E


def _t5_tiles(table):
    return pl.pallas_call(
        _t5_tiles_kernel,
        out_shape=jax.ShapeDtypeStruct((5, N_BIAS_HEADS, BIAS_TILE, BIAS_TILE), F32),
        grid=(N_BIAS_HEADS,),
        in_specs=[pl.BlockSpec(memory_space=pltpu.SMEM)],
        out_specs=pl.BlockSpec((5, 1, BIAS_TILE, BIAS_TILE), lambda h: (0, h, 0, 0)),
        compiler_params=_params("parallel"),
        name="t5_tiles",
    )(table)


def _na_block_delta(block_type, key_row, query_row):
    if block_type == 0:
        return key_row - query_row if key_row < NA_KR else None
    if block_type == 1:
        delta = key_row - query_row - NA_KR // 2
        return delta if -(NA_KR // 2) <= delta < NA_KR // 2 else None
    return key_row - query_row - NA_KR if key_row >= NA_SLAB_ROWS - NA_KR else None


def _na_bias_kernel(rpb_ref, o_ref):
    head = pl.program_id(0)
    n_col = 2 * NA_KC - 1
    shape = (GRID_W, NA_BLOCK_Q)
    kc = lax.broadcasted_iota(jnp.int32, shape, 0)
    col = lax.broadcasted_iota(jnp.int32, shape, 1)
    c = col & (GRID_W - 1)
    block_row = col >> int(math.log2(GRID_W))
    start = jnp.clip(c - NA_KC // 2, 0, GRID_W - NA_KC)
    valid = (kc >= start) & (kc < start + NA_KC)
    col_off = kc - c + (NA_KC - 1)
    masked = jnp.full(shape, MASK_VALUE, F32)
    per_row_offset = []
    for ro in range(2 * NA_KR - 1):
        out = masked
        for co in range(n_col):
            out = jnp.where(valid & (col_off == co), rpb_ref[head, ro * n_col + co] * LOG2E, out)
        per_row_offset.append(out)
    for block_type in range(3):
        for key_row in range(NA_SLAB_ROWS):
            piece = masked
            for query_row in range(NA_BLOCK_ROWS):
                delta = _na_block_delta(block_type, key_row, query_row)
                if delta is not None:
                    piece = jnp.where(block_row == query_row, per_row_offset[delta + NA_KR - 1], piece)
            o_ref[0, block_type, key_row * GRID_W:(key_row + 1) * GRID_W, :] = piece


def _na_bias(rpb):
    n_row = 2 * NA_KR - 1
    n_col = 2 * NA_KC - 1
    return pl.pallas_call(
        _na_bias_kernel,
        out_shape=jax.ShapeDtypeStruct((NA_HEADS, 3, NA_SLAB_Q, NA_BLOCK_Q), F32),
        grid=(NA_HEADS,),
        in_specs=[pl.BlockSpec(memory_space=pltpu.SMEM)],
        out_specs=pl.BlockSpec((1, 3, NA_SLAB_Q, NA_BLOCK_Q), lambda h: (h, 0, 0, 0)),
        compiler_params=_params("parallel"),
        name="na_bias",
    )(rpb.reshape(NA_HEADS, n_row * n_col))


def _dense_attn_kernel(q_ref, k_ref, vt_ref, bias_ref, vec_ref, o_ref, qm_ref, sa_ref, sb_ref, stat_ref, acc_ref,
                       *, diff, lam_init, n_qb):
    qi = pl.program_id(2)
    n_kb = vt_ref.shape[0]
    tq = q_ref.shape[0]
    sub_k = ATT_TK // BIAS_TILE
    dv = DIFF_V if diff else MLA_V
    run_max, max_a, max_b = 0, 2, 4

    q = q_ref[...]
    if diff:
        lane = lax.broadcasted_iota(jnp.int32, q.shape, 1)
        zero = jnp.zeros_like(q)
        qm_ref[0] = jnp.where(lane < DIFF_QK, q, zero).T
        qm_ref[1] = jnp.where(lane >= DIFF_QK, q, zero).T
    else:
        qm_ref[0] = q[:, :LANES].T
        qm_ref[1] = q[:, LANES:].T
    stat_ref[...] = jnp.full(stat_ref.shape, MASK_VALUE, F32)
    acc_ref[...] = jnp.zeros(acc_ref.shape, F32)
    ones = jnp.ones((ONES_ROWS, ATT_TK), BF16)

    tile_rows = lambda a: slice(a * BIAS_TILE, (a + 1) * BIAS_TILE)
    side_const = lambda t: (bias_ref[0, t, 0:1, 0:1], bias_ref[4, t, 0:1, 0:1])

    def scores(j, buf, off):
        s_ref, mb_row = buf
        kb = k_ref[pl.ds(pl.multiple_of(j * ATT_TK, ATT_TK), ATT_TK), :]
        for t in range(2):
            kt = kb if diff else kb[:, t * LANES:(t + 1) * LANES]
            c_left, c_right = side_const(t)
            for st in range(tq // ATT_STRIP):
                cols = slice(st * ATT_STRIP, (st + 1) * ATT_STRIP)
                s = _dot(kt, qm_ref[t, :, cols])
                if off is None:
                    s_ref[t, :, cols] = s
                    stat_ref[mb_row + t, :, cols] = jnp.max(s, axis=0, keepdims=True)
                    continue
                for bb in range(ATT_STRIP // BIAS_TILE):
                    b = st * (ATT_STRIP // BIAS_TILE) + bb
                    best = None
                    for a in range(sub_k):
                        piece = s[tile_rows(a), tile_rows(bb)]
                        o = off(a, b)
                        if abs(o) <= 1:
                            piece = piece + bias_ref[o + 2, t]
                            piece_max = jnp.max(piece, axis=0, keepdims=True)
                        else:
                            piece_max = jnp.max(piece, axis=0, keepdims=True) + (c_left if o < 0 else c_right)
                        s_ref[t, tile_rows(a), tile_rows(b)] = piece
                        best = piece_max if best is None else jnp.maximum(best, piece_max)
                    stat_ref[mb_row + t, :, tile_rows(b)] = best

    def update(j, buf, left, off):
        s_ref, mb_row = buf
        vb = vt_ref[j]
        for t in range(2):
            c_left, c_right = side_const(t)
            vv = vb if diff else vb[t * MLA_V:(t + 1) * MLA_V]
            lhs = jnp.concatenate([vv, ones], axis=0)
            for st in range(tq // ATT_STRIP):
                cols = slice(st * ATT_STRIP, (st + 1) * ATT_STRIP)
                m_prev = stat_ref[run_max + t, :, cols]
                if off is None:
                    c = jnp.where(left, c_left, c_right)
                    m_new = jnp.maximum(m_prev, stat_ref[mb_row + t, :, cols] + c)
                    p = jnp.exp2(s_ref[t, :, cols] - (m_new - c)).astype(BF16)
                else:
                    m_new = jnp.maximum(m_prev, stat_ref[mb_row + t, :, cols])
                    p_cols = []
                    for bb in range(ATT_STRIP // BIAS_TILE):
                        b = st * (ATT_STRIP // BIAS_TILE) + bb
                        m_b = m_new[:, tile_rows(bb)]
                        shift = {-1: m_b - c_left, 0: m_b, 1: m_b - c_right}
                        p_cols.append(jnp.concatenate(
                            [jnp.exp2(s_ref[t, tile_rows(a), tile_rows(b)]
                                      - shift[0 if abs(off(a, b)) <= 1 else (1 if off(a, b) > 0 else -1)])
                             for a in range(sub_k)], axis=0))
                    p = jnp.concatenate(p_cols, axis=1).astype(BF16)
                alpha = jnp.exp2(m_prev - m_new)
                acc_ref[t, :, cols] = acc_ref[t, :, cols] * alpha + _dot(lhs, p)
                stat_ref[run_max + t, :, cols] = m_new

    bufs = ((sa_ref, max_a), (sb_ref, max_b))
    n_near = tq // ATT_TK + 2
    n_far = n_kb - n_near
    w0 = jnp.clip(qi * (tq // ATT_TK) - 1, 0, n_far)
    far_block = lambda i: jnp.where(i < w0, i, i + n_near)
    far_buf = lambda parity: bufs[(n_near + parity) % 2]

    def near_phase(delta):
        def tile_offset(i):
            return lambda a, b: (i + delta) * sub_k + a - b

        def near_step(i):
            if i < n_near:
                scores(w0 + i, bufs[i % 2], tile_offset(i))
            elif n_far:
                scores(far_block(0), far_buf(0), None)
            if i > 0:
                update(w0 + i - 1, bufs[(i - 1) % 2], None, tile_offset(i - 1))

        one_trip = jnp.minimum(qi + 1, 1)

        def region(first, last):
            def body(_, carry):
                for i in range(first, last):
                    near_step(i)
                return carry
            lax.fori_loop(0, one_trip, body, 0)

        region(0, 3)
        for first in range(3, n_near + 1, 2):
            region(first, min(first + 2, n_near + 1))

    pl.when(qi == 0)(lambda: near_phase(0))
    pl.when(qi == n_qb - 1)(lambda: near_phase(-2))
    if n_qb > 2:
        pl.when((qi > 0) & (qi < n_qb - 1))(lambda: near_phase(-1))
    if n_far:

        def far_step(i, parity):
            j_prev = far_block(i - 1)
            scores(far_block(i), far_buf(parity), None)
            update(j_prev, far_buf(1 - parity), j_prev < w0, None)

        def far_trip(trip, carry):
            for u in range(1, FAR_PER_TRIP + 1):
                far_step(FAR_PER_TRIP * trip + u, u % 2)
            return carry

        n_trips = (n_far - 1) // FAR_PER_TRIP
        lax.fori_loop(0, n_trips, far_trip, 0)
        for i in range(FAR_PER_TRIP * n_trips + 1, n_far):
            far_step(i, i % 2)
        j_last = far_block(n_far - 1)
        update(j_last, far_buf((n_far - 1) % 2), j_last < w0, None)

    a0 = acc_ref[0]
    a1 = acc_ref[1]
    o0 = a0[:dv] / a0[dv:dv + 1]
    o1 = a1[:dv] / a1[dv:dv + 1]
    if diff:
        lam_dot = lambda r: jnp.sum(vec_ref[r:r + 1, :DIFF_QK] * vec_ref[r + 1:r + 2, :DIFF_QK], axis=-1, keepdims=True)
        lam = jnp.exp(lam_dot(0)) - jnp.exp(lam_dot(2)) + lam_init
        o = (o0 - lam * o1).T
        o = _rms(o, vec_ref[4:5, :]) * (1.0 - lam_init)
    else:
        o = jnp.concatenate([o0, o1], axis=0).T
    o_ref[...] = o.astype(BF16)


def _dense_attn(q, k, vt, bias, vecs, batch, seq_len, diff, lam_init):
    pw = PAIR_W if diff else 2 * LANES
    dv = DIFF_V if diff else MLA_V
    tq = min(ATT_TQ_MAX, seq_len // 2)
    nq = seq_len // tq
    nk = seq_len // ATT_TK
    assert tq % ATT_TK == 0 and seq_len % tq == 0 and nk >= tq // ATT_TK + 2, "the near window must fit"
    return pl.pallas_call(
        functools.partial(_dense_attn_kernel, diff=diff, lam_init=lam_init, n_qb=nq),
        out_shape=jax.ShapeDtypeStruct((batch * seq_len, D_MODEL), BF16),
        grid=(batch, N_BIAS_HEADS // 2, nq),
        in_specs=[
            pl.BlockSpec((tq, pw), lambda b, h, i: (b * nq + i, h)),
            pl.BlockSpec((seq_len, pw), lambda b, h, i: (b, h)),
            pl.BlockSpec((nk, PAIR_W, ATT_TK), lambda b, h, i: (b, h, 0)),
            pl.BlockSpec((5, 2, BIAS_TILE, BIAS_TILE), lambda b, h, i: (0, h, 0, 0)),
            pl.BlockSpec(vecs.shape, lambda b, h, i: (0, 0)),
        ],
        out_specs=pl.BlockSpec((tq, PAIR_W), lambda b, h, i: (b * nq + i, h)),
        scratch_shapes=[
            pltpu.VMEM((2, LANES, tq), BF16),
            pltpu.VMEM((2, ATT_TK, tq), F32),
            pltpu.VMEM((2, ATT_TK, tq), F32),
            pltpu.VMEM((8, 1, tq), F32),
            pltpu.VMEM((2, dv + ONES_ROWS, tq), F32),
        ],
        compiler_params=_params("parallel", "parallel", "parallel"),
        name="diff_attn" if diff else "mla_attn",
    )(q, k, vt, bias, vecs)


def _na_attn_kernel(q_ref, k_ref, vt_ref, bias_ref, o_ref, sa_ref, sb_ref, stat_ref):
    n_blocks = q_ref.shape[0] // NA_BLOCK_Q
    chunks = NA_SLAB_Q // NA_BLOCK_Q
    lane = lax.broadcasted_iota(jnp.int32, (NA_BLOCK_Q, PAIR_W), 1)
    first = lane < NA_HEAD_DIM
    ones = jnp.ones((ONES_ROWS, NA_SLAB_Q), BF16)
    slab_chunk = lambda g: jnp.clip(g - 1, 0, n_blocks - chunks)

    def scores(g, buf):
        s_ref, mb_row = buf
        block_type = jnp.where(g == 0, 0, jnp.where(g == n_blocks - 1, 2, 1))
        q = q_ref[pl.ds(pl.multiple_of(g * NA_BLOCK_Q, NA_BLOCK_Q), NA_BLOCK_Q), :]
        ks = k_ref[pl.ds(pl.multiple_of(slab_chunk(g) * NA_BLOCK_Q, NA_BLOCK_Q), NA_SLAB_Q), :]
        zero = jnp.zeros_like(q)
        for t in range(2):
            qt = jnp.where(first if t == 0 else ~first, q, zero)
            s = _dot_nt(ks, qt) + bias_ref[t, block_type]
            s_ref[t] = s
            stat_ref[mb_row + t] = jnp.max(s, axis=0, keepdims=True)

    def finish(g, buf):
        s_ref, mb_row = buf
        c0 = slab_chunk(g)
        vts = jnp.concatenate([vt_ref[c0 + i] for i in range(chunks)], axis=1)
        outs = []
        for t in range(2):
            p = jnp.exp2(s_ref[t] - stat_ref[mb_row + t]).astype(BF16)
            vv = vts[t * NA_HEAD_DIM:(t + 1) * NA_HEAD_DIM]
            acc = _dot(jnp.concatenate([vv, ones], axis=0), p)
            outs.append(acc[:NA_HEAD_DIM] / acc[NA_HEAD_DIM:NA_HEAD_DIM + 1])
        o = jnp.concatenate(outs, axis=0).T
        o_ref[pl.ds(pl.multiple_of(g * NA_BLOCK_Q, NA_BLOCK_Q), NA_BLOCK_Q), :] = o.astype(BF16)

    buf_a = (sa_ref, 0)
    buf_b = (sb_ref, 2)
    scores(0, buf_a)

    def two_blocks(pair, carry):
        g = 2 * pair
        scores(g + 1, buf_b)
        finish(g, buf_a)
        scores(g + 2, buf_a)
        finish(g + 1, buf_b)
        return carry

    lax.fori_loop(0, n_blocks // 2 - 1, two_blocks, 0)
    scores(n_blocks - 1, buf_b)
    finish(n_blocks - 2, buf_a)
    finish(n_blocks - 1, buf_b)


def _na_attn(q, k, vt, bias, batch, seq_len):
    n_blocks = seq_len // NA_BLOCK_Q
    assert n_blocks % 2 == 0 and n_blocks * NA_BLOCK_ROWS >= NA_SLAB_ROWS
    seq = pl.BlockSpec((seq_len, PAIR_W), lambda h, b: (b, h))
    return pl.pallas_call(
        _na_attn_kernel,
        out_shape=jax.ShapeDtypeStruct((batch * seq_len, D_MODEL), BF16),
        grid=(NA_HEADS // 2, batch),
        in_specs=[seq, seq,
                  pl.BlockSpec((n_blocks, PAIR_W, NA_BLOCK_Q), lambda h, b: (b, h, 0)),
                  pl.BlockSpec((2, 3, NA_SLAB_Q, NA_BLOCK_Q), lambda h, b: (h, 0, 0, 0))],
        out_specs=seq,
        scratch_shapes=[
            pltpu.VMEM((2, NA_SLAB_Q, NA_BLOCK_Q), F32),
            pltpu.VMEM((2, NA_SLAB_Q, NA_BLOCK_Q), F32),
            pltpu.VMEM((16, 1, NA_BLOCK_Q), F32),
        ],
        compiler_params=_params("parallel", "parallel"),
        name="na_attn",
    )(q, k, vt, bias)


def _pad_heads(w, heads, dim):
    w = w.reshape(w.shape[0], heads, dim)
    return jnp.pad(w, ((0, 0), (0, 0), (0, LANES - dim))).reshape(w.shape[0], heads * LANES)


def _rot_half_cols(w_rope):
    half = w_rope.shape[-1] // 2
    return jnp.concatenate([-w_rope[..., half:], w_rope[..., :half]], axis=-1)


def _rope_tables(max_len):
    half = MLA_ROPE // 2
    freqs = ROPE_THETA ** (-jnp.arange(half, dtype=F32) / half)
    ang = jnp.arange(max_len, dtype=F32)[:, None] * freqs[None, :]
    cos = jnp.cos(ang)
    sin = jnp.sin(ang)
    pad = jnp.zeros((max_len, LANES - MLA_NOPE - MLA_ROPE), F32)
    cos_t = jnp.concatenate([jnp.ones((max_len, MLA_NOPE), F32), cos, cos, pad], axis=-1)
    sin_t = jnp.concatenate([jnp.zeros((max_len, MLA_NOPE), F32), sin, sin, pad], axis=-1)
    return cos_t, sin_t


def _mla_weights(w_dq, g_q, w_uq, w_dkv, g_kv, w_uk, w_uv, max_len):
    d_qk = MLA_NOPE + MLA_ROPE
    uq = w_uq.reshape(MLA_Q_LORA, MLA_HEADS, d_qk)
    uq_rot = jnp.concatenate(
        [jnp.zeros((MLA_Q_LORA, MLA_HEADS, MLA_NOPE), F32), _rot_half_cols(uq[..., MLA_NOPE:])], axis=-1)
    w_kr = w_dkv[:, MLA_KV_LORA:]
    place = lambda w: jnp.pad(w, ((0, 0), (MLA_NOPE, LANES - d_qk)))
    cos_t, sin_t = _rope_tables(max_len)
    return {
        "wdq": w_dq.astype(BF16),
        "gq": g_q.reshape(1, -1),
        "wuq": _pad_heads(uq.reshape(MLA_Q_LORA, -1), MLA_HEADS, d_qk).astype(BF16),
        "wuq_rot": _pad_heads(uq_rot.reshape(MLA_Q_LORA, -1), MLA_HEADS, d_qk).astype(BF16),
        "wckv": w_dkv[:, :MLA_KV_LORA].astype(BF16),
        "gkv": g_kv.reshape(1, -1),
        "wkr": place(w_kr).astype(BF16),
        "wkr_rot": place(_rot_half_cols(w_kr)).astype(BF16),
        "wuk": _pad_heads(w_uk, MLA_HEADS, MLA_NOPE).astype(BF16),
        "wuvt": w_uv.T.astype(BF16),
        "cos": cos_t,
        "sin": sin_t,
    }


def _ffn_weights(w_gate, w_up, w_down):
    return w_gate.astype(BF16), w_up.astype(BF16), w_down.astype(BF16)


def _diff_vecs(lam_q1, lam_k1, lam_q2, lam_k2, g_sub):
    rows = [jnp.pad(v, (0, LANES - v.shape[0])) for v in (lam_q1, lam_k1, lam_q2, lam_k2)] + [g_sub]
    return jnp.pad(jnp.stack(rows), ((0, VEC_ROWS - len(rows)), (0, 0)))


def _trunk(x, p):
    batch, seq_len, _ = x.shape
    x = x.reshape(batch * seq_len, D_MODEL)
    no_vecs = jnp.zeros((VEC_ROWS, LANES), F32)
    for i in range(DEPTH):
        x = _ffn(x, None, p["norm_g"][i, 0].reshape(1, -1), *p["ffn"][i][0], p["final_g"], False)
        g_mix = p["norm_g"][i, 1].reshape(1, -1)
        m, j = i % N_MIXERS, i // N_MIXERS
        if m == 0:
            w = p["mla"][j]
            q, k, vt = _mla_proj(x, seq_len, g_mix, w)
            a = _dense_attn(q, k, vt, p["t5"], no_vecs, batch, seq_len, False, 0.0)
        elif m == 1:
            w = p["diff"][j]
            lam_init = 0.8 - 0.6 * math.exp(-0.3 * i)
            q, k, vt = _qkv_proj(x, g_mix, w["wq"], w["wk"], w["wvt"], DIFF_QK ** -0.5 * LOG2E, ATT_TK)
            a = _dense_attn(q, k, vt, p["t5"], w["vecs"], batch, seq_len, True, lam_init)
        else:
            w = p["na"][j]
            q, k, vt = _qkv_proj(x, g_mix, w["wq"], w["wk"], w["wvt"], NA_HEAD_DIM ** -0.5 * LOG2E, NA_BLOCK_Q)
            a = _na_attn(q, k, vt, w["bias"], batch, seq_len)
        x = _ffn(x, (a, w["wo"]), p["norm_g"][i, 2].reshape(1, -1), *p["ffn"][i][1], p["final_g"], i == DEPTH - 1)
    return x.reshape(batch, seq_len, D_MODEL)


def kernel(x_prompt, x_sample, norm_g, final_g, ffn_w_gate, ffn_w_up, ffn_w_down, rel_bias_table, mla_w_dq, mla_g_q, mla_w_uq, mla_w_dkv, mla_g_kv, mla_w_uk, mla_w_uv, mla_w_o, diff_w_q, diff_w_k, diff_w_v, diff_lam_q1, diff_lam_k1, diff_lam_q2, diff_lam_k2, diff_g_sub, diff_w_o, na_w_qkv, na_rpb, na_w_o):
    max_len = max(x_prompt.shape[1], x_sample.shape[1])
    hd = NA_HEADS * NA_HEAD_DIM
    p = {
        "norm_g": norm_g,
        "final_g": final_g.reshape(1, -1),
        "ffn": [[_ffn_weights(ffn_w_gate[i, s], ffn_w_up[i, s], ffn_w_down[i, s]) for s in range(2)]
                for i in range(DEPTH)],
        "t5": _t5_tiles(rel_bias_table),
        "mla": [dict(_mla_weights(mla_w_dq[j], mla_g_q[j], mla_w_uq[j], mla_w_dkv[j], mla_g_kv[j],
                                  mla_w_uk[j], mla_w_uv[j], max_len), wo=mla_w_o[j].astype(BF16))
                for j in range(mla_w_dq.shape[0])],
        "diff": [{
            "wq": diff_w_q[j].astype(BF16), "wk": diff_w_k[j].astype(BF16), "wvt": diff_w_v[j].T.astype(BF16),
            "vecs": _diff_vecs(diff_lam_q1[j], diff_lam_k1[j], diff_lam_q2[j], diff_lam_k2[j], diff_g_sub[j]),
            "wo": diff_w_o[j].astype(BF16),
        } for j in range(diff_w_q.shape[0])],
        "na": [{
            "wq": na_w_qkv[j][:, :hd].astype(BF16), "wk": na_w_qkv[j][:, hd:2 * hd].astype(BF16),
            "wvt": na_w_qkv[j][:, 2 * hd:].T.astype(BF16), "bias": _na_bias(na_rpb[j]),
            "wo": na_w_o[j].astype(BF16),
        } for j in range(na_w_qkv.shape[0])],
    }
    return (_trunk(x_prompt, p), _trunk(x_sample, p))
```
